```python
import jax
import jax.numpy as jnp
from jax import lax
import numpy as np

D_MODEL = 2048
BATCH = 4
SEQ = 4096
DEPTH = 4

GRID_W = 64
CTX_LEN = 256
D_POOL = 1024
POOL_WINDOWS = (2, 4, 8, 16)
N_POOL_GROUPS = 4
POOL_GROUP = D_POOL // N_POOL_GROUPS
HG_HEADS = 8
HG_DK = 128
HG_DV = 128
HG_WIDTH = HG_HEADS * HG_DK
CHUNK = 64
N_BRANCH = 2
IN_COLS = D_POOL + 5 * HG_WIDTH + N_BRANCH * D_MODEL
N_EXPERTS = 16
N_EXPERT_GROUPS = 4
EXPERTS_PER_GROUP = N_EXPERTS // N_EXPERT_GROUPS
TOP_K = 2
D_EXPERT = 512
MOE_BLOCK = 128
EPS = 1e-6

kernel_name = 'hybrid_pool_hgrn2_moe_dit'


def rmsnorm(x, w):
    x32 = x.astype(jnp.float32)
    y = x32 * lax.rsqrt(jnp.mean(x32 * x32, axis=-1, keepdims=True) + EPS)
    return (y * w.astype(jnp.float32)).astype(x.dtype)


def window_mean(u, w):
    L = u.shape[2]
    u32 = u.astype(jnp.float32)
    cs = jnp.concatenate([jnp.zeros_like(u32[:, :, :1]), jnp.cumsum(u32, axis=2)], axis=2)
    t = jnp.arange(L)
    lo = jnp.maximum(t - w // 2, 0)
    hi = jnp.minimum(t + w // 2 - 1, L - 1)
    s = jnp.take(cs, hi + 1, axis=2) - jnp.take(cs, lo, axis=2)
    cnt = (hi - lo + 1).astype(jnp.float32)
    return (s / cnt[:, None]).astype(u.dtype)


def pool_mixer(u, w_pg, pool_scale):
    outs = []
    for g, w in enumerate(POOL_WINDOWS):
        ug = u[..., g * POOL_GROUP:(g + 1) * POOL_GROUP]
        outs.append(jnp.einsum('brlc,cd->brld', window_mean(ug, w) - ug, w_pg[g]))
    return jnp.concatenate(outs, axis=-1) * pool_scale


def heads(a):
    b, t = a.shape[:2]
    return a.reshape(b, t, HG_HEADS, -1).transpose(0, 2, 1, 3).astype(jnp.float32)


def hgrn2_inputs(zq, zf_fwd, zf_bwd, zi, lb_fwd, lb_bwd):
    q = heads(jax.nn.silu(zq.astype(jnp.float32)))
    v = heads(zi)

    def gate(zf, lb):
        f = lb + (1.0 - lb) * jax.nn.sigmoid(zf.astype(jnp.float32))
        return heads(1.0 - f), heads(jnp.log(f))

    k_f, lf_f = gate(zf_fwd, lb_fwd)
    k_b, lf_b = gate(zf_bwd, lb_bwd)
    stack = lambda a, b: jnp.stack([a, jnp.flip(b, axis=2)])
    return stack(q, q), stack(k_f, k_b), stack(lf_f, lf_b), stack(v, v)


def hgrn2_scan(q, k, log_f, v, s0):
    T = q.shape[3]
    n = T // CHUNK

    def to_chunks(a):
        return jnp.moveaxis(a.reshape(a.shape[:3] + (n, CHUNK, a.shape[-1])), 3, 0)

    lower = jnp.tril(jnp.ones((CHUNK, CHUNK), dtype=bool))

    def step(S, inp):
        qc, kc, lfc, vc = inp
        b = jnp.cumsum(lfc, axis=-2)
        diff = b[..., :, None, :] - b[..., None, :, :]
        dec = jnp.exp(jnp.where(lower[:, :, None], diff, -jnp.inf))
        attn = jnp.einsum('dbhtk,dbhtsk,dbhsk->dbhts', qc, dec, kc)
        o = (jnp.einsum('dbhtk,dbhkv->dbhtv', qc * jnp.exp(b), S)
             + jnp.einsum('dbhts,dbhsv->dbhtv', attn, vc))
        bl = b[..., -1:, :]
        S = (jnp.exp(bl[..., 0, :])[..., None] * S
             + jnp.einsum('dbhsk,dbhsv->dbhkv', kc * jnp.exp(bl - b), vc))
        return S, o

    sT, o = lax.scan(step, s0, (to_chunks(q), to_chunks(k), to_chunks(log_f), to_chunks(v)))
    o = jnp.moveaxis(o, 0, 3).reshape(v.shape)
    return o, sT


def hgrn2_readout(o2, og, w_norm):
    o = o2[0] + jnp.flip(o2[1], axis=2)
    o = o * lax.rsqrt(jnp.mean(o * o, axis=-1, keepdims=True) + EPS)
    b, _, t, _ = o.shape
    o = o.transpose(0, 2, 1, 3).reshape(b, t, HG_WIDTH)
    return (o * w_norm.astype(jnp.float32) * jax.nn.silu(og.astype(jnp.float32))).astype(og.dtype)


def split_in(z):
    cuts = [D_POOL + j * HG_WIDTH for j in range(6)] + [D_POOL + 5 * HG_WIDTH + D_MODEL]
    return jnp.split(z, cuts, axis=-1)


def branch_merge(p, o2, og, gp, gh, grid, w_pg, pool_scale, hg_norm_w, w_bp, w_bh, w_out):
    b, t, _ = p.shape
    pool = pool_mixer(p.reshape((b,) + grid + (D_POOL,)), w_pg, pool_scale).reshape(b, t, D_POOL)
    hg = hgrn2_readout(o2, og, hg_norm_w)
    m = jax.nn.sigmoid(gp) * (pool @ w_bp) + jax.nn.sigmoid(gh) * (hg @ w_bh)
    return m @ w_out


def token_mixers(hl, hc, w_in, w_pg, pool_scale, hg_norm_w, lb_fwd, lb_bwd, w_bp, w_bh, w_out, need_ctx):
    b, n, _ = hl.shape
    rows = n // GRID_W
    pl, ql, ffl, fbl, il, ogl, gpl, ghl = split_in(hl @ w_in)
    pc, qc, ffc, fbc, ic, ogc, gpc, ghc = split_in(hc @ w_in)
    s0 = jnp.zeros((2, b, HG_HEADS, HG_DK, HG_DV), jnp.float32)
    o_ctx, s_ctx = hgrn2_scan(*hgrn2_inputs(qc, ffc, fbc, ic, lb_fwd, lb_bwd), s0)
    o_lat, _ = hgrn2_scan(*hgrn2_inputs(ql, ffl, fbl, il, lb_fwd, lb_bwd), s_ctx)
    yl = branch_merge(pl, o_lat, ogl, gpl, ghl, (rows, GRID_W), w_pg, pool_scale, hg_norm_w, w_bp, w_bh, w_out)
    if not need_ctx:
        return yl, None
    yc = branch_merge(pc, o_ctx, ogc, gpc, ghc, (1, hc.shape[1]), w_pg, pool_scale, hg_norm_w, w_bp, w_bh, w_out)
    return yl, yc


def moe_ffn(h, w_router, router_bias, w1, w3, w2):
    T, D = h.shape
    scores = jax.nn.sigmoid(h.astype(jnp.float32) @ w_router.astype(jnp.float32))
    sel = scores + router_bias.astype(jnp.float32)
    grp = sel.reshape(T, N_EXPERT_GROUPS, EXPERTS_PER_GROUP)
    grp_score = lax.top_k(grp, 2)[0].sum(-1)
    g_idx = jnp.argmax(grp_score, axis=-1)
    in_grp = jnp.take_along_axis(grp, g_idx[:, None, None], axis=1)[:, 0]
    _, loc = lax.top_k(in_grp, TOP_K)
    e_idx = (g_idx[:, None] * EXPERTS_PER_GROUP + loc).astype(jnp.int32)
    gate = jnp.take_along_axis(scores, e_idx, axis=1)
    gate = gate / jnp.sum(gate, axis=-1, keepdims=True)
    n_assign = T * TOP_K
    flat_e = e_idx.reshape(-1)
    flat_tok = jnp.repeat(jnp.arange(T, dtype=jnp.int32), TOP_K)
    order = jnp.argsort(flat_e)
    se, stok, sgate = flat_e[order], flat_tok[order], gate.reshape(-1)[order]
    counts = jnp.bincount(flat_e, length=N_EXPERTS)
    starts = jnp.cumsum(counts) - counts
    pcounts = (counts + MOE_BLOCK - 1) // MOE_BLOCK * MOE_BLOCK
    pends = jnp.cumsum(pcounts)
    dest = (pends - pcounts)[se] + jnp.arange(n_assign) - starts[se]
    n_blocks = -(-n_assign // MOE_BLOCK) + N_EXPERTS
    buf_tok = jnp.zeros((n_blocks * MOE_BLOCK,), jnp.int32).at[dest].set(stok)
    blk_e = jnp.minimum(jnp.searchsorted(pends, jnp.arange(n_blocks) * MOE_BLOCK, side='right'), N_EXPERTS - 1)
    xb = h[buf_tok].reshape(n_blocks, MOE_BLOCK, D)

    def expert_block(args):
        xblk, e = args
        return (jax.nn.silu(xblk @ w1[e]) * (xblk @ w3[e])) @ w2[e]

    yb = lax.map(expert_block, (xb, blk_e)).reshape(n_blocks * MOE_BLOCK, D)
    return jnp.zeros_like(h).at[stok].add(yb[dest] * sgate[:, None].astype(h.dtype))


def setup_inputs(seed: int = 0) -> dict:
    key = jax.random.key(seed)
    ks = jax.random.split(key, 22)
    D = D_MODEL
    nrm = lambda k, shape, s: jax.random.normal(k, shape, jnp.float32) * s
    return {
        'x': nrm(ks[0], (BATCH, SEQ, D), 1.0),
        'c': nrm(ks[1], (BATCH, D), 1.0),
        'ctx': nrm(ks[2], (BATCH, CTX_LEN, D), 1.0),
        'c_ctx': nrm(ks[3], (D,), 1.0),
        'w_ada': nrm(ks[4], (DEPTH, D, 6 * D), 0.5 * D ** -0.5),
        'b_ada': nrm(ks[5], (DEPTH, 6 * D), 0.01),
        'norm1_w': 1.0 + nrm(ks[6], (DEPTH, D), 0.02),
        'norm2_w': 1.0 + nrm(ks[7], (DEPTH, D), 0.02),
        'w_in': nrm(ks[8], (DEPTH, D, IN_COLS), D ** -0.5),
        'w_pool_group': nrm(ks[9], (DEPTH, N_POOL_GROUPS, POOL_GROUP, POOL_GROUP), POOL_GROUP ** -0.5),
        'pool_scale': 1.0 + nrm(ks[10], (DEPTH, D_POOL), 0.02),
        'hg_norm_w': 1.0 + nrm(ks[11], (DEPTH, HG_WIDTH), 0.02),
        'lb_logits': nrm(ks[12], (2, DEPTH, HG_WIDTH), 0.1),
        'w_branch_pool': nrm(ks[13], (DEPTH, D_POOL, D), D_POOL ** -0.5),
        'w_branch_hgrn': nrm(ks[14], (DEPTH, HG_WIDTH, D), HG_WIDTH ** -0.5),
        'w_out': nrm(ks[15], (DEPTH, D, D), D ** -0.5),
        'w_router': nrm(ks[16], (D, N_EXPERTS), D ** -0.5),
        'router_bias': nrm(ks[17], (N_EXPERTS,), 0.01),
        'w_e1': nrm(ks[18], (DEPTH, N_EXPERTS, D, D_EXPERT), D ** -0.5),
        'w_e3': nrm(ks[19], (DEPTH, N_EXPERTS, D, D_EXPERT), D ** -0.5),
        'w_e2': nrm(ks[20], (DEPTH, N_EXPERTS, D_EXPERT, D), D_EXPERT ** -0.5),
        'final_norm_w': 1.0 + nrm(ks[21], (D,), 0.02),
    }


def reference(x, c, ctx, c_ctx, w_ada, b_ada, norm1_w, norm2_w, w_in, w_pool_group, pool_scale,
              hg_norm_w, lb_logits, w_branch_pool, w_branch_hgrn, w_out, w_router, router_bias,
              w_e1, w_e3, w_e2, final_norm_w):
    b, n, d = x.shape
    sc = jax.nn.silu(c)
    scc = jax.nn.silu(c_ctx)
    lbw = jax.nn.softmax(lb_logits.astype(jnp.float32), axis=1)
    lb = jnp.cumsum(lbw, axis=1) - lbw[:, :1]
    xl, xc = x, ctx
    for i in range(DEPTH):
        last = i == DEPTH - 1
        mod_l = jnp.split((sc @ w_ada[i] + b_ada[i])[:, None, :], 6, axis=-1)
        mod_c = jnp.split(scc @ w_ada[i] + b_ada[i], 6, axis=-1)
        hl = rmsnorm(xl, norm1_w[i]) * (1 + mod_l[1]) + mod_l[0]
        hc = rmsnorm(xc, norm1_w[i]) * (1 + mod_c[1]) + mod_c[0]
        yl, yc = token_mixers(hl, hc, w_in[i], w_pool_group[i], pool_scale[i], hg_norm_w[i],
                              lb[0, i], lb[1, i], w_branch_pool[i], w_branch_hgrn[i], w_out[i], not last)
        xl = xl + mod_l[2] * yl
        hl2 = rmsnorm(xl, norm2_w[i]) * (1 + mod_l[4]) + mod_l[3]
        if last:
            f = moe_ffn(hl2.reshape(-1, d), w_router, router_bias, w_e1[i], w_e3[i], w_e2[i])
            xl = xl + mod_l[5] * f.reshape(xl.shape)
        else:
            xc = xc + mod_c[2] * yc
            hc2 = rmsnorm(xc, norm2_w[i]) * (1 + mod_c[4]) + mod_c[3]
            tokens = jnp.concatenate([hl2.reshape(-1, d), hc2.reshape(-1, d)], axis=0)
            f = moe_ffn(tokens, w_router, router_bias, w_e1[i], w_e3[i], w_e2[i])
            xl = xl + mod_l[5] * f[:b * n].reshape(xl.shape)
            xc = xc + mod_c[5] * f[b * n:].reshape(xc.shape)
    return rmsnorm(xl, final_norm_w)
```

```python
import functools
import math

import numpy as np
import jax
import jax.numpy as jnp
from jax import lax
from jax.experimental import pallas as pl
from jax.experimental.pallas import tpu as pltpu

F32 = jnp.float32
BF16 = jnp.bfloat16

EPS = 1e-6
GRID_W = 64
POOL_WINDOWS = (2, 4, 8, 16)
HEAD_DIM = 128
N_EXPERT_GROUPS = 4
EXPERTS_PER_GROUP = 4
SCAN_CHUNK = 64
ROW_TILE = 256
MOE_ROWS = 256
V7X_VMEM_BYTES = 64 * 1024 * 1024
VMEM_LIMIT = V7X_VMEM_BYTES - 8 * 1024 * 1024


def _cparams(n_axes, vmem=VMEM_LIMIT):
    return pltpu.CompilerParams(dimension_semantics=("arbitrary",) * n_axes, vmem_limit_bytes=vmem)


def _sigmoid(x):
    return 1.0 / (1.0 + jnp.exp(-x))


def _dot(a, b):
    return jnp.dot(a, b, preferred_element_type=F32)


def _dot_nt(a, b):
    return lax.dot_general(a, b, (((1,), (1,)), ((), ())), preferred_element_type=F32)


def _lb_kernel(depth, logit_ref, lb_ref):
    for d in range(2):
        rows = [logit_ref[d * depth + i:d * depth + i + 1, :] for i in range(depth)]
        m = rows[0]
        for r in rows[1:]:
            m = jnp.maximum(m, r)
        es = [jnp.exp(r - m) for r in rows]
        tot = es[0]
        for e in es[1:]:
            tot = tot + e
        ws = [e / tot for e in es]
        acc = jnp.zeros_like(ws[0])
        for i in range(depth):
            acc = acc + ws[i]
            lb_ref[d * depth + i:d * depth + i + 1, :] = acc - ws[0]


def _lower_bounds(lb_logits):
    _, depth, hw = lb_logits.shape
    flat = lb_logits.astype(F32).reshape(2 * depth, hw)
    out = pl.pallas_call(
        functools.partial(_lb_kernel, depth),
        name="lower_bounds",
        out_shape=jax.ShapeDtypeStruct((2 * depth, hw), F32),
    )(flat)
    return out.reshape(2 * depth, 1, hw)


def _ada_kernel(c_ref, w_ref, b_ref, o_ref):
    c = c_ref[...]
    s = (c * _sigmoid(c)).astype(BF16)
    o_ref[0] = _dot(s, w_ref[0].astype(BF16)) + b_ref[0]


def _ada_mods(cc, w_ada, b_ada):
    depth, d, n6 = w_ada.shape
    tn = 1024
    return pl.pallas_call(
        _ada_kernel,
        name="ada_mods",
        grid=(depth, n6 // tn),
        in_specs=[
            pl.BlockSpec((8, d), lambda l, j: (0, 0)),
            pl.BlockSpec((1, d, tn), lambda l, j: (l, 0, j)),
            pl.BlockSpec((1, 1, tn), lambda l, j: (l, 0, j)),
        ],
        out_specs=pl.BlockSpec((1, 8, tn), lambda l, j: (l, 0, j)),
        out_shape=jax.ShapeDtypeStruct((depth, 8, n6), F32),
        compiler_params=_cparams(2),
    )(cc, w_ada, b_ada.reshape(depth, 1, n6))


def _inproj_kernel(d, n_ctx, tiles_per_sample, ctx_row, rc, x_ref, mod_ref, nw_ref, w_ref, o_ref, h_ref):
    r = pl.program_id(0)
    j = pl.program_id(1)
    tm = x_ref.shape[0]

    @pl.when(j == 0)
    def _():
        b = r // tiles_per_sample
        row0 = (r % tiles_per_sample) * tm
        sh_l = mod_ref[pl.ds(b, 1), 0:d]
        sc_l = mod_ref[pl.ds(b, 1), d:2 * d]
        sh_c = mod_ref[ctx_row:ctx_row + 1, 0:d]
        sc_c = mod_ref[ctx_row:ctx_row + 1, d:2 * d]
        nw = nw_ref[...]

        def body(ci, carry):
            off = pl.multiple_of(ci * rc, rc)
            x = x_ref[pl.ds(off, rc), :]
            n = x * lax.rsqrt(jnp.mean(x * x, axis=-1, keepdims=True) + EPS) * nw
            rows = row0 + off + lax.broadcasted_iota(jnp.int32, (rc, 1), 0)
            h = jnp.where(rows < n_ctx, n * (1.0 + sc_c) + sh_c, n * (1.0 + sc_l) + sh_l)
            h_ref[pl.ds(off, rc), :] = h.astype(BF16)
            return carry

        lax.fori_loop(0, tm // rc, body, 0)

    o_ref[...] = _dot(h_ref[...], w_ref[...]).astype(BF16)


def _in_proj(x, mods_l, norm_w, w_in_bf, n_ctx, t_sample, ctx_row, tm):
    ntot, d = x.shape
    n_cols = w_in_bf.shape[1]
    tn = 512
    rc = 64 if tm % 64 == 0 else 16
    return pl.pallas_call(
        functools.partial(_inproj_kernel, d, n_ctx, t_sample // tm, ctx_row, rc),
        name="in_proj",
        grid=(ntot // tm, n_cols // tn),
        in_specs=[
            pl.BlockSpec((tm, d), lambda r, j: (r, 0)),
            pl.BlockSpec((8, 2 * d), lambda r, j: (0, 0)),
            pl.BlockSpec((1, d), lambda r, j: (0, 0)),
            pl.BlockSpec((d, tn), lambda r, j: (0, j)),
        ],
        out_specs=pl.BlockSpec((tm, tn), lambda r, j: (r, j)),
        out_shape=jax.ShapeDtypeStruct((ntot, n_cols), BF16),
        scratch_shapes=[pltpu.VMEM((tm, d), BF16)],
        compiler_params=_cparams(2),
    )(x, mods_l, norm_w.reshape(1, d), w_in_bf)


def _scan_levels(c):
    m = c // 2
    out = []
    while m >= 1:
        out.append(m)
        m //= 2
    return out


def _scan_constants(c):
    levels = _scan_levels(c)
    t = np.arange(c)[:, None]
    r = np.arange(c)[None, :]
    sets = [(r <= t), (r > t), np.ones((8, c), bool)]
    masks = [np.eye(c, dtype=bool)]
    for m in levels:
        mid = (t // (2 * m)) * (2 * m) + m
        sets.append((t >= mid) & (r >= mid) & (r <= t))
        sets.append((t < mid) & (r > t) & (r <= mid - 1))
        same = (t // (2 * m)) == (r // (2 * m))
        masks.append(same & (t >= mid) & (r < mid))
    fwd_m = np.concatenate(sets, axis=0).astype(np.float32)
    fwd_k = np.stack(masks).astype(np.float32)
    bwd_sets = [s[::-1, ::-1] if s.shape[0] == c else s for s in sets]
    bwd_m = np.concatenate(bwd_sets, axis=0).astype(np.float32)
    bwd_k = fwd_k[:, ::-1, ::-1]
    return np.stack([fwd_m, bwd_m]), np.stack([fwd_k, bwd_k])


def _scan_kernel(n_heads, c, q_ref, f_ref, i_ref, lb_ref, m_ref, k_ref, o_ref, s_ref, e_ref):
    step = pl.program_id(2)
    levels = _scan_levels(c)

    @pl.when(step == 0)
    def _():
        s_ref[...] = jnp.zeros_like(s_ref)

    zq = q_ref[...].astype(F32)
    zf = f_ref[...].astype(F32)
    lbv = lb_ref[0]
    f = lbv + (1.0 - lbv) * _sigmoid(zf)
    lf = jnp.log(f)
    kk = 1.0 - f
    qq = zq * _sigmoid(zq)
    lf_hi = lf.astype(BF16)
    lf_lo = (lf - lf_hi.astype(F32)).astype(BF16)
    mm = m_ref[0]
    e_ref[...] = jnp.exp(_dot(mm, lf_hi) + _dot(mm, lf_lo))

    lvl0 = 2 * c + 8
    for h in range(n_heads):
        cs = slice(h * HEAD_DIM, (h + 1) * HEAD_DIM)
        qh = qq[:, cs]
        kh = kk[:, cs]
        vh = i_ref[:, cs]
        a = k_ref[0, 0] * _dot_nt(qh.astype(BF16), kh.astype(BF16))
        for li in range(len(levels)):
            eq = e_ref[lvl0 + 2 * li * c:lvl0 + (2 * li + 1) * c, cs]
            ek = e_ref[lvl0 + (2 * li + 1) * c:lvl0 + (2 * li + 2) * c, cs]
            a = a + k_ref[0, li + 1] * _dot_nt((qh * eq).astype(BF16), (kh * ek).astype(BF16))
        st = s_ref[h]
        o = _dot_nt((qh * e_ref[0:c, cs]).astype(BF16), st.astype(BF16)) + _dot(a.astype(BF16), vh)
        o_ref[0, :, cs] = o.astype(BF16)
        vt = vh.astype(F32).T.astype(BF16)
        kt = (kh * e_ref[c:2 * c, cs]).astype(BF16)
        s_ref[h] = st * e_ref[2 * c:2 * c + 1, cs] + _dot(vt, kt)


def _scan(z, lb, layer, depth, n_batch, n_ctx, n_lat, hw):
    c = SCAN_CHUNK
    n_heads = hw // HEAD_DIM
    nc, nl = n_ctx // c, n_lat // c
    steps = nc + nl
    mats, masks = _scan_constants(c)
    mats = jnp.asarray(mats, BF16)
    masks = jnp.asarray(masks, F32)
    n_rows = mats.shape[1]
    ntot = z.shape[0]

    def chunk(b, d, s):
        bwd = jnp.where(s < nc, nc - 1 - s, nc + (steps - 1 - s))
        return b * steps + jnp.where(d == 0, s, bwd)

    return pl.pallas_call(
        functools.partial(_scan_kernel, n_heads, c),
        name="hgrn2_scan",
        grid=(n_batch, 2, steps),
        in_specs=[
            pl.BlockSpec((c, hw), lambda b, d, s: (chunk(b, d, s), 1)),
            pl.BlockSpec((c, hw), lambda b, d, s: (chunk(b, d, s), 2 + d)),
            pl.BlockSpec((c, hw), lambda b, d, s: (chunk(b, d, s), 4)),
            pl.BlockSpec((1, 1, hw), lambda b, d, s: (d * depth + layer, 0, 0)),
            pl.BlockSpec((1, n_rows, c), lambda b, d, s: (d, 0, 0)),
            pl.BlockSpec((1, masks.shape[1], c, c), lambda b, d, s: (d, 0, 0, 0)),
        ],
        out_specs=pl.BlockSpec((1, c, hw), lambda b, d, s: (d, chunk(b, d, s), 0)),
        out_shape=jax.ShapeDtypeStruct((2, ntot, hw), BF16),
        scratch_shapes=[pltpu.VMEM((n_heads, HEAD_DIM, HEAD_DIM), F32), pltpu.VMEM((n_rows, hw), F32)],
        compiler_params=_cparams(3),
    )(z, z, z, lb, mats, masks)


def _pool_constants(tile, n_ctx):
    assert n_ctx == tile and tile % GRID_W == 0
    band = np.zeros((2, len(POOL_WINDOWS), tile, tile), np.float32)
    inv = np.zeros((2, len(POOL_WINDOWS), tile, 1), np.float32)
    for kind, row_len in enumerate((n_ctx, GRID_W)):
        t = np.arange(tile)
        tau = t % row_len
        base = t - tau
        for g, w in enumerate(POOL_WINDOWS):
            lo = np.maximum(tau - w // 2, 0)
            hi = np.minimum(tau + w // 2 - 1, row_len - 1)
            s = np.arange(tile)[None, :]
            band[kind, g] = (s >= (base + lo)[:, None]) & (s <= (base + hi)[:, None])
            inv[kind, g, :, 0] = 1.0 / (hi - lo + 1)
    return band, inv


def _route(sel, sc):
    n_g, per = N_EXPERT_GROUPS, EXPERTS_PER_GROUP
    row = lambda a, e: a[e:e + 1, :]
    best = None
    for g in range(n_g):
        rows = [row(sel, g * per + i) for i in range(per)]
        gs = None
        for i in range(per):
            for j in range(i + 1, per):
                p = rows[i] + rows[j]
                gs = p if gs is None else jnp.maximum(gs, p)
        if best is None:
            best, g_idx = gs, jnp.zeros(gs.shape, jnp.int32)
        else:
            upd = gs > best
            g_idx = jnp.where(upd, g, g_idx)
            best = jnp.where(upd, gs, best)
    cand, raw = [], []
    for i in range(per):
        ci, ri = row(sel, i), row(sc, i)
        for g in range(1, n_g):
            ci = jnp.where(g_idx == g, row(sel, g * per + i), ci)
            ri = jnp.where(g_idx == g, row(sc, g * per + i), ri)
        cand.append(ci)
        raw.append(ri)

    def argmax4(vals):
        m, loc = vals[0], jnp.zeros(vals[0].shape, jnp.int32)
        for i in range(1, per):
            upd = vals[i] > m
            loc = jnp.where(upd, i, loc)
            m = jnp.where(upd, vals[i], m)
        return loc

    loc0 = argmax4(cand)
    loc1 = argmax4([jnp.where(loc0 == i, -jnp.inf, cand[i]) for i in range(per)])

    def pick(loc):
        out = raw[0]
        for i in range(1, per):
            out = jnp.where(loc == i, raw[i], out)
        return out

    g0, g1 = pick(loc0), pick(loc1)
    tot = g0 + g1
    return g_idx * per + loc0, g_idx * per + loc1, g0 / tot, g1 / tot


def _merge_kernel(d, hw, tiles_per_sample, ctx_row,
                  x_ref, p_ref, og_ref, gp_ref, gh_ref, o_ref, mod_ref, n2_ref, ps_ref, hn_ref,
                  band_ref, inv_ref, wpg_ref, wbp_ref, wbh_ref, wout_ref, wr_ref, rb_ref,
                  xo_ref, h2_ref, eidx_ref, gate_ref, pool_scr, hg_scr):
    i = pl.program_id(0)
    n_heads = hw // HEAD_DIM
    n_groups = len(POOL_WINDOWS)
    pg = p_ref.shape[1] // n_groups
    mrow = jnp.where(i % tiles_per_sample == 0, ctx_row, i // tiles_per_sample)
    mod = lambda k: mod_ref[pl.ds(mrow, 1), k * d:(k + 1) * d]

    for h in range(n_heads):
        cs = slice(h * HEAD_DIM, (h + 1) * HEAD_DIM)
        o = o_ref[0, :, cs].astype(F32) + o_ref[1, :, cs].astype(F32)
        o = o * lax.rsqrt(jnp.mean(o * o, axis=-1, keepdims=True) + EPS)
        og = og_ref[:, cs].astype(F32)
        hg_scr[:, cs] = (o * hn_ref[:, cs] * (og * _sigmoid(og))).astype(BF16)

    for g in range(n_groups):
        cs = slice(g * pg, (g + 1) * pg)
        u = p_ref[:, cs]
        wsum = _dot(band_ref[0, g], u)
        dlt = wsum * inv_ref[0, g] - u.astype(F32)
        pool_scr[:, cs] = (_dot(dlt.astype(BF16), wpg_ref[g]) * ps_ref[:, cs]).astype(BF16)

    gp = gp_ref[...].astype(F32)
    gh = gh_ref[...].astype(F32)
    m = _sigmoid(gp) * _dot(pool_scr[...], wbp_ref[...]) + _sigmoid(gh) * _dot(hg_scr[...], wbh_ref[...])
    y = _dot(m.astype(BF16), wout_ref[...])
    xn = x_ref[...] + mod(2) * y
    xo_ref[...] = xn
    n2 = xn * lax.rsqrt(jnp.mean(xn * xn, axis=-1, keepdims=True) + EPS) * n2_ref[...]
    h2 = n2 * (1.0 + mod(4)) + mod(3)
    h2_ref[...] = h2

    h_hi = h2.astype(BF16)
    h_lo = (h2 - h_hi.astype(F32)).astype(BF16)
    wr = wr_ref[...]
    w_hi = wr.astype(BF16)
    w_lo = (wr - w_hi.astype(F32)).astype(BF16)
    logits = _dot_nt(w_hi, h_hi) + (_dot_nt(w_hi, h_lo) + _dot_nt(w_lo, h_hi))
    sc = _sigmoid(logits)
    e0, e1, g0, g1 = _route(sc + rb_ref[...], sc)
    eidx_ref[...] = jnp.zeros_like(eidx_ref)
    gate_ref[...] = jnp.zeros_like(gate_ref)
    eidx_ref[0, 0:1, :] = e0
    eidx_ref[0, 1:2, :] = e1
    gate_ref[0, 0:1, :] = g0
    gate_ref[0, 1:2, :] = g1


def _const_spec(shape):
    nd = len(shape)
    return pl.BlockSpec(shape, lambda i: (0,) * nd, pipeline_mode=pl.Buffered(1))


def _merge(x, z, o2, mods_l, norm2_w, pool_scale, hg_norm_w, w_pg, w_bp, w_bh, w_out, w_router_t,
           router_bias, n_ctx, t_sample, ctx_row):
    ntot, d = x.shape
    hw = o2.shape[2]
    d_pool = w_bp.shape[0]
    assert d_pool == hw and d == 2 * hw
    tm = ROW_TILE
    tps = t_sample // tm
    n_e = w_router_t.shape[0]
    band, inv = _pool_constants(tm, n_ctx)
    band = jnp.asarray(band, BF16)
    inv = jnp.asarray(inv, F32)
    kind = lambda i: jnp.where(i % tps == 0, 0, 1)
    n_tiles = ntot // tm
    return pl.pallas_call(
        functools.partial(_merge_kernel, d, hw, tps, ctx_row),
        name="merge",
        grid=(n_tiles,),
        in_specs=[
            pl.BlockSpec((tm, d), lambda i: (i, 0)),
            pl.BlockSpec((tm, hw), lambda i: (i, 0)),
            pl.BlockSpec((tm, hw), lambda i: (i, 5)),
            pl.BlockSpec((tm, d), lambda i: (i, 3)),
            pl.BlockSpec((tm, d), lambda i: (i, 4)),
            pl.BlockSpec((2, tm, hw), lambda i: (0, i, 0)),
            _const_spec((8, 6 * d)),
            _const_spec((1, d)),
            _const_spec((1, hw)),
            _const_spec((1, hw)),
            pl.BlockSpec((1,) + band.shape[1:], lambda i: (kind(i), 0, 0, 0)),
            pl.BlockSpec((1,) + inv.shape[1:], lambda i: (kind(i), 0, 0, 0)),
            _const_spec(w_pg.shape),
            _const_spec(w_bp.shape),
            _const_spec(w_bh.shape),
            _const_spec(w_out.shape),
            _const_spec(w_router_t.shape),
            _const_spec((n_e, 1)),
        ],
        out_specs=[
            pl.BlockSpec((tm, d), lambda i: (i, 0)),
            pl.BlockSpec((tm, d), lambda i: (i, 0)),
            pl.BlockSpec((1, 8, tm), lambda i: (i, 0, 0)),
            pl.BlockSpec((1, 8, tm), lambda i: (i, 0, 0)),
        ],
        out_shape=[
            jax.ShapeDtypeStruct((ntot, d), F32),
            jax.ShapeDtypeStruct((ntot, d), F32),
            jax.ShapeDtypeStruct((n_tiles, 8, tm), jnp.int32),
            jax.ShapeDtypeStruct((n_tiles, 8, tm), F32),
        ],
        scratch_shapes=[pltpu.VMEM((tm, hw), BF16), pltpu.VMEM((tm, hw), BF16)],
        compiler_params=_cparams(1),
    )(x, z, z, z, z, o2, mods_l, norm2_w.reshape(1, d), pool_scale.reshape(1, hw),
      hg_norm_w.reshape(1, hw), band, inv, w_pg, w_bp, w_bh, w_out, w_router_t,
      router_bias.astype(F32).reshape(n_e, 1))


def _row_gather_start(src_hbm, idx_ref, dst, sem, n_rows):
    def body(r, carry):
        tok = idx_ref[0, 0, r]
        pltpu.make_async_copy(src_hbm.at[pl.ds(tok, 1)], dst.at[pl.ds(r, 1)], sem).start()
        return carry

    lax.fori_loop(0, n_rows, body, 0, unroll=8)


def _row_gather_wait(src_hbm, dst, sem, n_rows):
    pltpu.make_async_copy(src_hbm.at[pl.ds(0, n_rows)], dst, sem).wait()


def _expert_kernel(blk_e_ref, nused_ref, idx0_ref, idxn_ref, x_hbm, w1_ref, w3_ref, w2_ref,
                   y_ref, xbuf, sem, w1c, w3c, w2c):
    i = pl.program_id(0)
    n_used = nused_ref[0]
    rows = xbuf.shape[1]
    slot = i % 2

    @pl.when(i == 0)
    def _():
        _row_gather_start(x_hbm, idx0_ref, xbuf.at[0], sem.at[0], rows)

    @pl.when(i + 1 < n_used)
    def _():
        _row_gather_start(x_hbm, idxn_ref, xbuf.at[1 - slot], sem.at[1 - slot], rows)

    @pl.when(i < n_used)
    def _():
        changed = jnp.logical_or(i == 0, blk_e_ref[i] != blk_e_ref[jnp.maximum(i - 1, 0)])

        @pl.when(changed)
        def _():
            w1c[...] = w1_ref[0].astype(BF16)
            w3c[...] = w3_ref[0].astype(BF16)
            w2c[...] = w2_ref[0].astype(BF16)

        _row_gather_wait(x_hbm, xbuf.at[slot], sem.at[slot], rows)
        xb = xbuf[slot].astype(BF16)
        h1 = _dot(xb, w1c[...])
        h3 = _dot(xb, w3c[...])
        act = (h1 * _sigmoid(h1) * h3).astype(BF16)
        y_ref[...] = _dot(act, w2c[...])

    @pl.when(i >= n_used)
    def _():
        y_ref[...] = jnp.zeros_like(y_ref)


def _experts(h2, buf_tok, blk_e, n_used, w1, w3, w2):
    ntot, d = h2.shape
    n_e, _, f = w1.shape
    n_blk = blk_e.shape[0]
    rows = MOE_ROWS
    idx = buf_tok.reshape(n_blk, 1, rows)
    grid_spec = pltpu.PrefetchScalarGridSpec(
        num_scalar_prefetch=2,
        grid=(n_blk,),
        in_specs=[
            pl.BlockSpec((1, 1, rows), lambda i, be, nu: (0, 0, 0), memory_space=pltpu.SMEM),
            pl.BlockSpec((1, 1, rows), lambda i, be, nu: (jnp.minimum(i + 1, n_blk - 1), 0, 0),
                         memory_space=pltpu.SMEM),
            pl.BlockSpec(memory_space=pl.ANY),
            pl.BlockSpec((1, d, f), lambda i, be, nu: (be[i], 0, 0)),
            pl.BlockSpec((1, d, f), lambda i, be, nu: (be[i], 0, 0)),
            pl.BlockSpec((1, f, d), lambda i, be, nu: (be[i], 0, 0)),
        ],
        out_specs=pl.BlockSpec((rows, d), lambda i, be, nu: (i, 0)),
        scratch_shapes=[
            pltpu.VMEM((2, rows, d), F32),
            pltpu.SemaphoreType.DMA((2,)),
            pltpu.VMEM((d, f), BF16),
            pltpu.VMEM((d, f), BF16),
            pltpu.VMEM((f, d), BF16),
        ],
    )
    return pl.pallas_call(
        _expert_kernel,
        name="experts",
        grid_spec=grid_spec,
        out_shape=jax.ShapeDtypeStruct((n_blk * rows, d), F32),
        compiler_params=_cparams(1),
    )(blk_e, n_used, idx, idx, h2, w1, w3, w2)


def _combine_kernel(d, tiles_per_sample, ctx_row, pos0_ref, posn_ref, y_hbm, x_ref, g_ref, mod_ref,
                    xo_ref, rbuf, sem):
    i = pl.program_id(0)
    n = pl.num_programs(0)
    tm = x_ref.shape[0]
    slot = i % 2

    @pl.when(i == 0)
    def _():
        _row_gather_start(y_hbm, pos0_ref, rbuf.at[0], sem.at[0], 2 * tm)

    @pl.when(i + 1 < n)
    def _():
        _row_gather_start(y_hbm, posn_ref, rbuf.at[1 - slot], sem.at[1 - slot], 2 * tm)

    _row_gather_wait(y_hbm, rbuf.at[slot], sem.at[slot], 2 * tm)
    mrow = jnp.where(i % tiles_per_sample == 0, ctx_row, i // tiles_per_sample)
    m5 = mod_ref[pl.ds(mrow, 1), 5 * d:6 * d]
    g = g_ref[...]
    f = g[:, 0:1] * rbuf[slot, 0:tm, :] + g[:, 1:2] * rbuf[slot, tm:2 * tm, :]
    xo_ref[...] = x_ref[...] + m5 * f


def _combine(x, yb, pos, gates, mods_l, t_sample, ctx_row):
    ntot, d = x.shape
    tm = ROW_TILE
    n_tiles = ntot // tm
    return pl.pallas_call(
        functools.partial(_combine_kernel, d, t_sample // tm, ctx_row),
        name="combine",
        grid=(n_tiles,),
        in_specs=[
            pl.BlockSpec((1, 1, 2 * tm), lambda i: (0, 0, 0), memory_space=pltpu.SMEM),
            pl.BlockSpec((1, 1, 2 * tm), lambda i: (jnp.minimum(i + 1, n_tiles - 1), 0, 0),
                         memory_space=pltpu.SMEM),
            pl.BlockSpec(memory_space=pl.ANY),
            pl.BlockSpec((tm, d), lambda i: (i, 0)),
            pl.BlockSpec((tm, 2), lambda i: (i, 0)),
            _const_spec((8, 6 * d)),
        ],
        out_specs=pl.BlockSpec((tm, d), lambda i: (i, 0)),
        out_shape=jax.ShapeDtypeStruct((ntot, d), F32),
        scratch_shapes=[pltpu.VMEM((2, 2 * tm, d), F32), pltpu.SemaphoreType.DMA((2,))],
        compiler_params=_cparams(1),
    )(pos, pos, yb, x, gates, mods_l)


def _final_kernel(x_ref, w_ref, o_ref):
    x = x_ref[...]
    o_ref[...] = x * lax.rsqrt(jnp.mean(x * x, axis=-1, keepdims=True) + EPS) * w_ref[...]


def _final_norm(x, w, n_batch, n_ctx, n_lat):
    ntot, d = x.shape
    tm = ROW_TILE
    tps = (n_ctx + n_lat) // tm
    cps = n_ctx // tm
    lps = n_lat // tm
    return pl.pallas_call(
        _final_kernel,
        name="final_norm",
        grid=(n_batch * lps,),
        in_specs=[
            pl.BlockSpec((tm, d), lambda i: ((i // lps) * tps + cps + i % lps, 0)),
            pl.BlockSpec((1, d), lambda i: (0, 0)),
        ],
        out_specs=pl.BlockSpec((tm, d), lambda i: (i, 0)),
        out_shape=jax.ShapeDtypeStruct((n_batch * n_lat, d), F32),
        compiler_params=_cparams(1),
    )(x, w.reshape(1, d))


def _dispatch_plan(eidx, gate, ntot, n_e):
    rows = MOE_ROWS
    n_tiles, _, tm = eidx.shape
    e = jnp.transpose(eidx[:, 0:2, :], (0, 2, 1)).reshape(ntot, 2)
    g = jnp.transpose(gate[:, 0:2, :], (0, 2, 1)).reshape(ntot, 2)
    flat_e = e.reshape(-1)
    n_assign = flat_e.shape[0]
    onehot = (flat_e[:, None] == jnp.arange(n_e, dtype=jnp.int32)[None, :]).astype(jnp.int32)
    counts = onehot.sum(axis=0)
    rank = jnp.sum((jnp.cumsum(onehot, axis=0) - onehot) * onehot, axis=1)
    pcounts = (counts + rows - 1) // rows * rows
    pends = jnp.cumsum(pcounts)
    dest = (pends - pcounts)[flat_e] + rank
    n_blk = -(-n_assign // rows) + n_e
    tok = jnp.arange(n_assign, dtype=jnp.int32) // 2
    buf_tok = jnp.zeros((n_blk * rows,), jnp.int32).at[dest].set(tok)
    blk_e = jnp.minimum(
        jnp.searchsorted(pends, jnp.arange(n_blk, dtype=jnp.int32) * rows, side='right'), n_e - 1
    ).astype(jnp.int32)
    n_used = (pends[-1] // rows).astype(jnp.int32).reshape(1)
    pos = jnp.transpose(dest.astype(jnp.int32).reshape(n_tiles, tm, 2), (0, 2, 1)).reshape(n_tiles, 1, 2 * tm)
    return buf_tok, blk_e, n_used, pos, g


def _inproj_rows(t_sample):
    best = 16
    for k in range(16, 1089, 16):
        if t_sample % k == 0:
            best = k
    return best


def kernel(x, c, ctx, c_ctx, w_ada, b_ada, norm1_w, norm2_w, w_in, w_pool_group, pool_scale, hg_norm_w,
           lb_logits, w_branch_pool, w_branch_hgrn, w_out, w_router, router_bias, w_e1, w_e3, w_e2,
           final_norm_w):
    n_batch, n_lat, d = x.shape
    n_ctx = ctx.shape[1]
    depth = w_ada.shape[0]
    hw = hg_norm_w.shape[1]
    n_e = w_router.shape[1]
    t_sample = n_ctx + n_lat
    ntot = n_batch * t_sample
    ctx_row = n_batch
    assert n_batch < 8 and n_ctx == ROW_TILE and n_lat % ROW_TILE == 0

    xs = jnp.concatenate([ctx, x], axis=1).reshape(ntot, d)
    cc = jnp.zeros((8, d), F32).at[:n_batch].set(c).at[ctx_row].set(c_ctx)
    mods = _ada_mods(cc, w_ada, b_ada)
    lb = _lower_bounds(lb_logits)
    w_router_t = jnp.transpose(w_router.astype(F32))
    tm_in = _inproj_rows(t_sample)

    for l in range(depth):
        z = _in_proj(xs, mods[l], norm1_w[l], w_in[l].astype(BF16), n_ctx, t_sample, ctx_row, tm_in)
        o2 = _scan(z, lb, l, depth, n_batch, n_ctx, n_lat, hw)
        xs, h2, eidx, gate = _merge(
            xs, z, o2, mods[l], norm2_w[l], pool_scale[l], hg_norm_w[l], w_pool_group[l].astype(BF16),
            w_branch_pool[l].astype(BF16), w_branch_hgrn[l].astype(BF16), w_out[l].astype(BF16),
            w_router_t, router_bias, n_ctx, t_sample, ctx_row)
        buf_tok, blk_e, n_used, pos, g = _dispatch_plan(eidx, gate, ntot, n_e)
        yb = _experts(h2, buf_tok, blk_e, n_used, w_e1[l], w_e3[l], w_e2[l])
        xs = _combine(xs, yb, pos, g, mods[l], t_sample, ctx_row)

    out = _final_norm(xs, final_norm_w, n_batch, n_ctx, n_lat)
    return out.reshape(n_batch, n_lat, d)
```

```python
import functools
import math

import numpy as np
import jax
import jax.numpy as jnp
from jax import lax
from jax.experimental import pallas as pl
from jax.experimental.pallas import tpu as pltpu

F32 = jnp.float32
BF16 = jnp.bfloat16

EPS = 1e-6
GRID_W = 64
POOL_WINDOWS = (2, 4, 8, 16)
HEAD_DIM = 128
N_EXPERT_GROUPS = 4
EXPERTS_PER_GROUP = 4
SCAN_CHUNK = 64
ROW_TILE = 256
MOE_ROWS = 256
V7X_VMEM_BYTES = 64 * 1024 * 1024
VMEM_LIMIT = V7X_VMEM_BYTES - 8 * 1024 * 1024


def _cparams(n_axes, vmem=VMEM_LIMIT):
    return pltpu.CompilerParams(dimension_semantics=("arbitrary",) * n_axes, vmem_limit_bytes=vmem)


def _sigmoid(x):
    return 1.0 / (1.0 + jnp.exp(-x))


def _dot(a, b):
    return jnp.dot(a, b, preferred_element_type=F32)


def _dot_nt(a, b):
    return lax.dot_general(a, b, (((1,), (1,)), ((), ())), preferred_element_type=F32)


def _lb_kernel(depth, logit_ref, lb_ref):
    for d in range(2):
        rows = [logit_ref[d * depth + i:d * depth + i + 1, :] for i in range(depth)]
        m = rows[0]
        for r in rows[1:]:
            m = jnp.maximum(m, r)
        es = [jnp.exp(r - m) for r in rows]
        tot = es[0]
        for e in es[1:]:
            tot = tot + e
        ws = [e / tot for e in es]
        acc = jnp.zeros_like(ws[0])
        for i in range(depth):
            acc = acc + ws[i]
            lb_ref[d * depth + i:d * depth + i + 1, :] = acc - ws[0]


def _lower_bounds(lb_logits):
    _, depth, hw = lb_logits.shape
    flat = lb_logits.astype(F32).reshape(2 * depth, hw)
    out = pl.pallas_call(
        functools.partial(_lb_kernel, depth),
        name="lower_bounds",
        out_shape=jax.ShapeDtypeStruct((2 * depth, hw), F32),
    )(flat)
    return out.reshape(2 * depth, 1, hw)


def _ada_kernel(c_ref, w_ref, b_ref, o_ref):
    c = c_ref[...]
    s = (c * _sigmoid(c)).astype(BF16)
    o_ref[0] = _dot(s, w_ref[0].astype(BF16)) + b_ref[0]


def _ada_mods(cc, w_ada, b_ada):
    depth, d, n6 = w_ada.shape
    tn = 1024
    return pl.pallas_call(
        _ada_kernel,
        name="ada_mods",
        grid=(depth, n6 // tn),
        in_specs=[
            pl.BlockSpec((8, d), lambda l, j: (0, 0)),
            pl.BlockSpec((1, d, tn), lambda l, j: (l, 0, j)),
            pl.BlockSpec((1, 1, tn), lambda l, j: (l, 0, j)),
        ],
        out_specs=pl.BlockSpec((1, 8, tn), lambda l, j: (l, 0, j)),
        out_shape=jax.ShapeDtypeStruct((depth, 8, n6), F32),
        compiler_params=_cparams(2),
    )(cc, w_ada, b_ada.reshape(depth, 1, n6))


def _inproj_kernel(d, n_ctx, tiles_per_sample, ctx_row, rc, x_ref, mod_ref, nw_ref, w_ref, o_ref, h_ref):
    r = pl.program_id(0)
    j = pl.program_id(1)
    tm = x_ref.shape[0]

    @pl.when(j == 0)
    def _():
        b = r // tiles_per_sample
        row0 = (r % tiles_per_sample) * tm
        sh_l = mod_ref[pl.ds(b, 1), 0:d]
        sc_l = mod_ref[pl.ds(b, 1), d:2 * d]
        sh_c = mod_ref[ctx_row:ctx_row + 1, 0:d]
        sc_c = mod_ref[ctx_row:ctx_row + 1, d:2 * d]
        nw = nw_ref[...]

        def body(ci, carry):
            off = pl.multiple_of(ci * rc, rc)
            x = x_ref[pl.ds(off, rc), :]
            n = x * lax.rsqrt(jnp.mean(x * x, axis=-1, keepdims=True) + EPS) * nw
            rows = row0 + off + lax.broadcasted_iota(jnp.int32, (rc, 1), 0)
            h = jnp.where(rows < n_ctx, n * (1.0 + sc_c) + sh_c, n * (1.0 + sc_l) + sh_l)
            h_ref[pl.ds(off, rc), :] = h.astype(BF16)
            return carry

        lax.fori_loop(0, tm // rc, body, 0)

    o_ref[...] = _dot(h_ref[...], w_ref[0].astype(BF16)).astype(BF16)


def _in_proj(x, mods_l, norm_w, w_in, layer, n_ctx, t_sample, ctx_row, tm):
    ntot, d = x.shape
    n_cols = w_in.shape[2]
    tn = 1024
    rc = 64 if tm % 64 == 0 else 16
    return pl.pallas_call(
        functools.partial(_inproj_kernel, d, n_ctx, t_sample // tm, ctx_row, rc),
        name="in_proj",
        grid=(ntot // tm, n_cols // tn),
        in_specs=[
            pl.BlockSpec((tm, d), lambda r, j: (r, 0)),
            pl.BlockSpec((8, 2 * d), lambda r, j: (0, 0)),
            pl.BlockSpec((1, d), lambda r, j: (0, 0)),
            pl.BlockSpec((1, d, tn), lambda r, j: (layer, 0, j)),
        ],
        out_specs=pl.BlockSpec((tm, tn), lambda r, j: (r, j)),
        out_shape=jax.ShapeDtypeStruct((ntot, n_cols), BF16),
        scratch_shapes=[pltpu.VMEM((tm, d), BF16)],
        compiler_params=_cparams(2),
    )(x, mods_l, norm_w.reshape(1, d), w_in)


COMPACT_MIN_HALF = 8
MXU_LEVEL_MIN_HALF = 2


def _scan_levels(c):
    out, m = [], c // 2
    while m >= MXU_LEVEL_MIN_HALF:
        out.append(m)
        m //= 2
    return out


def _scan_row_layout(c):
    off = {"b": 0, "kend": c, "tot": 2 * c}
    pos = 2 * c + 8
    for m in _scan_levels(c):
        n = c // 2 if m >= COMPACT_MIN_HALF else c
        off[("q", m)] = pos
        off[("k", m)] = pos + n
        pos += 2 * n
    return off, pos


def _scan_constants(c):
    t = np.arange(c)[:, None]
    r = np.arange(c)[None, :]
    out = []
    for flip in (False, True):
        fl = (lambda a: a[::-1, ::-1]) if flip else (lambda a: a)
        sets = [fl(r <= t), fl(r > t), np.ones((8, c), bool)]
        for m in _scan_levels(c):
            mid = (t // (2 * m)) * (2 * m) + m
            qa = fl((t >= mid) & (r >= mid) & (r <= t))
            ka = fl((t < mid) & (r > t) & (r <= mid - 1))
            if m >= COMPACT_MIN_HALF:
                rows = np.arange(c)
                q_valid = (rows % (2 * m) >= m) != flip
                sets += [qa[q_valid], ka[~q_valid]]
            else:
                sets += [qa, ka]
        mat = np.concatenate(sets, axis=0).astype(np.float32)
        out.append(np.concatenate([mat, mat], axis=1))
    return np.stack(out)


def _tile_pairs(e, m, c):
    parts = []
    for p in range(c // (2 * m)):
        blk = e[p * m:(p + 1) * m]
        parts += [blk, blk]
    return jnp.concatenate(parts, axis=0)


def _scan_kernel(n_heads, c, q_ref, f_ref, i_ref, lb_ref, m_ref, o_ref, s_ref, e_ref, a_ref):
    d = pl.program_id(1)
    step = pl.program_id(2)
    levels = _scan_levels(c)
    off, _ = _scan_row_layout(c)

    @pl.when(step == 0)
    def _():
        s_ref[...] = jnp.zeros_like(s_ref)

    zq = q_ref[...].astype(F32)
    zf = f_ref[...].astype(F32)
    lbv = lb_ref[0]
    f = lbv + (1.0 - lbv) * _sigmoid(zf)
    kk = 1.0 - f
    qq = zq * _sigmoid(zq)
    lf2 = jnp.log(f) * math.log2(math.e)
    lf_hi = lf2.astype(BF16)
    lf_lo = (lf2 - lf_hi.astype(F32)).astype(BF16)
    e_ref[...] = jnp.exp2(_dot(m_ref[0], jnp.concatenate([lf_hi, lf_lo], axis=0)))

    row = lax.broadcasted_iota(jnp.int32, (c, 1), 0)
    k_sw = jnp.where((row & 1) == 1, pltpu.roll(kk, 1, axis=0), pltpu.roll(kk, c - 1, axis=0))
    qf = qq * f

    ti = lax.broadcasted_iota(jnp.int32, (c, c), 0)
    si = lax.broadcasted_iota(jnp.int32, (c, c), 1)
    same = {m: (ti & -(2 * m)) == (si & -(2 * m)) for m in levels[1:] + [1]}
    eye = ti == si
    causal = (ti - si) * (1 - 2 * d) >= 0

    for h in range(n_heads):
        cs = slice(h * HEAD_DIM, (h + 1) * HEAD_DIM)
        qh = qq[:, cs]
        kh = kk[:, cs]
        a = None
        for m in levels:
            n = c // 2 if m >= COMPACT_MIN_HALF else c
            eq = e_ref[off[("q", m)]:off[("q", m)] + n, cs]
            ek = e_ref[off[("k", m)]:off[("k", m)] + n, cs]
            if m >= COMPACT_MIN_HALF:
                eq, ek = _tile_pairs(eq, m, c), _tile_pairs(ek, m, c)
            p = _dot_nt((qh * eq).astype(BF16), (kh * ek).astype(BF16))
            a = p if a is None else jnp.where(same[m], p, a)
        pair = jnp.sum(qf[:, cs] * k_sw[:, cs], axis=-1, keepdims=True)
        diag = jnp.sum(qh * kh, axis=-1, keepdims=True)
        a = jnp.where(same[1], pair, a)
        a = jnp.where(eye, diag, a)
        a_ref[h] = jnp.where(causal, a, 0.0).astype(BF16)

    for h in range(n_heads):
        cs = slice(h * HEAD_DIM, (h + 1) * HEAD_DIM)
        qh = qq[:, cs]
        kh = kk[:, cs]
        vh = i_ref[:, cs]
        st = s_ref[h]
        qb = (qh * e_ref[off["b"]:off["b"] + c, cs]).astype(BF16)
        o = _dot_nt(qb, st.astype(BF16)) + _dot(a_ref[h], vh)
        o_ref[0, :, cs] = o.astype(BF16)
        vt = vh.astype(F32).T.astype(BF16)
        kt = (kh * e_ref[off["kend"]:off["kend"] + c, cs]).astype(BF16)
        s_ref[h] = st * e_ref[off["tot"]:off["tot"] + 1, cs] + _dot(vt, kt)


def _scan(z, lb, layer, depth, n_batch, n_ctx, n_lat, hw):
    c = SCAN_CHUNK
    n_heads = hw // HEAD_DIM
    nc, nl = n_ctx // c, n_lat // c
    steps = nc + nl
    mats = jnp.asarray(_scan_constants(c), BF16)
    n_rows = mats.shape[1]
    assert n_rows == _scan_row_layout(c)[1]
    ntot = z.shape[0]

    def chunk(b, d, s):
        bwd = jnp.where(s < nc, nc - 1 - s, nc + (steps - 1 - s))
        return b * steps + jnp.where(d == 0, s, bwd)

    return pl.pallas_call(
        functools.partial(_scan_kernel, n_heads, c),
        name="hgrn2_scan",
        grid=(n_batch, 2, steps),
        in_specs=[
            pl.BlockSpec((c, hw), lambda b, d, s: (chunk(b, d, s), 1)),
            pl.BlockSpec((c, hw), lambda b, d, s: (chunk(b, d, s), 2 + d)),
            pl.BlockSpec((c, hw), lambda b, d, s: (chunk(b, d, s), 4)),
            pl.BlockSpec((1, 1, hw), lambda b, d, s: (d * depth + layer, 0, 0)),
            pl.BlockSpec((1, n_rows, 2 * c), lambda b, d, s: (d, 0, 0)),
        ],
        out_specs=pl.BlockSpec((1, c, hw), lambda b, d, s: (d, chunk(b, d, s), 0)),
        out_shape=jax.ShapeDtypeStruct((2, ntot, hw), BF16),
        scratch_shapes=[pltpu.VMEM((n_heads, HEAD_DIM, HEAD_DIM), F32), pltpu.VMEM((n_rows, hw), F32),
                        pltpu.VMEM((n_heads, c, c), BF16)],
        compiler_params=_cparams(3),
    )(z, z, z, lb, mats)


def _pool_constants(tile, n_ctx):
    assert n_ctx == tile and tile % GRID_W == 0
    band = np.zeros((2, len(POOL_WINDOWS), tile, tile), np.float32)
    inv = np.zeros((2, len(POOL_WINDOWS), tile, 1), np.float32)
    for kind, row_len in enumerate((n_ctx, GRID_W)):
        t = np.arange(tile)
        tau = t % row_len
        base = t - tau
        for g, w in enumerate(POOL_WINDOWS):
            lo = np.maximum(tau - w // 2, 0)
            hi = np.minimum(tau + w // 2 - 1, row_len - 1)
            s = np.arange(tile)[None, :]
            band[kind, g] = (s >= (base + lo)[:, None]) & (s <= (base + hi)[:, None])
            inv[kind, g, :, 0] = 1.0 / (hi - lo + 1)
    return band, inv


def _route(sel, sc):
    n_g, per = N_EXPERT_GROUPS, EXPERTS_PER_GROUP
    row = lambda a, e: a[e:e + 1, :]
    best = None
    for g in range(n_g):
        rows = [row(sel, g * per + i) for i in range(per)]
        gs = None
        for i in range(per):
            for j in range(i + 1, per):
                p = rows[i] + rows[j]
                gs = p if gs is None else jnp.maximum(gs, p)
        if best is None:
            best, g_idx = gs, jnp.zeros(gs.shape, jnp.int32)
        else:
            upd = gs > best
            g_idx = jnp.where(upd, g, g_idx)
            best = jnp.where(upd, gs, best)
    cand, raw = [], []
    for i in range(per):
        ci, ri = row(sel, i), row(sc, i)
        for g in range(1, n_g):
            ci = jnp.where(g_idx == g, row(sel, g * per + i), ci)
            ri = jnp.where(g_idx == g, row(sc, g * per + i), ri)
        cand.append(ci)
        raw.append(ri)

    def argmax4(vals):
        m, loc = vals[0], jnp.zeros(vals[0].shape, jnp.int32)
        for i in range(1, per):
            upd = vals[i] > m
            loc = jnp.where(upd, i, loc)
            m = jnp.where(upd, vals[i], m)
        return loc

    loc0 = argmax4(cand)
    loc1 = argmax4([jnp.where(loc0 == i, -jnp.inf, cand[i]) for i in range(per)])

    def pick(loc):
        out = raw[0]
        for i in range(1, per):
            out = jnp.where(loc == i, raw[i], out)
        return out

    g0, g1 = pick(loc0), pick(loc1)
    tot = g0 + g1
    return g_idx * per + loc0, g_idx * per + loc1, g0 / tot, g1 / tot


def _merge_kernel(d, hw, tiles_per_sample, ctx_row,
                  x_ref, p_ref, og_ref, gp_ref, gh_ref, o_ref, mod_ref, n2_ref, ps_ref, hn_ref,
                  band_ref, inv_ref, wpg_ref, wbp_ref, wbh_ref, wout_ref, wr_ref, rb_ref,
                  xo_ref, h2_ref, eidx_ref, gate_ref, pool_scr, hg_scr):
    i = pl.program_id(0)
    n_heads = hw // HEAD_DIM
    n_groups = len(POOL_WINDOWS)
    pg = p_ref.shape[1] // n_groups
    mrow = jnp.where(i % tiles_per_sample == 0, ctx_row, i // tiles_per_sample)
    mod = lambda k: mod_ref[pl.ds(mrow, 1), k * d:(k + 1) * d]

    for h in range(n_heads):
        cs = slice(h * HEAD_DIM, (h + 1) * HEAD_DIM)
        o = o_ref[0, :, cs].astype(F32) + o_ref[1, :, cs].astype(F32)
        o = o * lax.rsqrt(jnp.mean(o * o, axis=-1, keepdims=True) + EPS)
        og = og_ref[:, cs].astype(F32)
        hg_scr[:, cs] = (o * hn_ref[:, cs] * (og * _sigmoid(og))).astype(BF16)

    for g in range(n_groups):
        cs = slice(g * pg, (g + 1) * pg)
        u = p_ref[:, cs]
        wsum = _dot(band_ref[0, g], u)
        dlt = wsum * inv_ref[0, g] - u.astype(F32)
        pool_scr[:, cs] = (_dot(dlt.astype(BF16), wpg_ref[g]) * ps_ref[:, cs]).astype(BF16)

    gp = gp_ref[...].astype(F32)
    gh = gh_ref[...].astype(F32)
    m = _sigmoid(gp) * _dot(pool_scr[...], wbp_ref[...]) + _sigmoid(gh) * _dot(hg_scr[...], wbh_ref[...])
    y = _dot(m.astype(BF16), wout_ref[...])
    xn = x_ref[...] + mod(2) * y
    xo_ref[...] = xn
    n2 = xn * lax.rsqrt(jnp.mean(xn * xn, axis=-1, keepdims=True) + EPS) * n2_ref[...]
    h2 = n2 * (1.0 + mod(4)) + mod(3)
    h2_ref[...] = h2

    h_hi = h2.astype(BF16)
    h_lo = (h2 - h_hi.astype(F32)).astype(BF16)
    wr = wr_ref[...]
    w_hi = wr.astype(BF16)
    w_lo = (wr - w_hi.astype(F32)).astype(BF16)
    logits = _dot_nt(w_hi, h_hi) + (_dot_nt(w_hi, h_lo) + _dot_nt(w_lo, h_hi))
    sc = _sigmoid(logits)
    e0, e1, g0, g1 = _route(sc + rb_ref[...], sc)
    eidx_ref[...] = jnp.zeros_like(eidx_ref)
    gate_ref[...] = jnp.zeros_like(gate_ref)
    eidx_ref[0, 0:1, :] = e0
    eidx_ref[0, 1:2, :] = e1
    gate_ref[0, 0:1, :] = g0
    gate_ref[0, 1:2, :] = g1


def _const_spec(shape):
    nd = len(shape)
    return pl.BlockSpec(shape, lambda i: (0,) * nd, pipeline_mode=pl.Buffered(1))


def _merge(x, z, o2, mods_l, norm2_w, pool_scale, hg_norm_w, w_pg, w_bp, w_bh, w_out, w_router_t,
           router_bias, n_ctx, t_sample, ctx_row):
    ntot, d = x.shape
    hw = o2.shape[2]
    d_pool = w_bp.shape[0]
    assert d_pool == hw and d == 2 * hw
    tm = ROW_TILE
    tps = t_sample // tm
    n_e = w_router_t.shape[0]
    band, inv = _pool_constants(tm, n_ctx)
    band = jnp.asarray(band, BF16)
    inv = jnp.asarray(inv, F32)
    kind = lambda i: jnp.where(i % tps == 0, 0, 1)
    n_tiles = ntot // tm
    return pl.pallas_call(
        functools.partial(_merge_kernel, d, hw, tps, ctx_row),
        name="merge",
        grid=(n_tiles,),
        in_specs=[
            pl.BlockSpec((tm, d), lambda i: (i, 0)),
            pl.BlockSpec((tm, hw), lambda i: (i, 0)),
            pl.BlockSpec((tm, hw), lambda i: (i, 5)),
            pl.BlockSpec((tm, d), lambda i: (i, 3)),
            pl.BlockSpec((tm, d), lambda i: (i, 4)),
            pl.BlockSpec((2, tm, hw), lambda i: (0, i, 0)),
            _const_spec((8, 6 * d)),
            _const_spec((1, d)),
            _const_spec((1, hw)),
            _const_spec((1, hw)),
            pl.BlockSpec((1,) + band.shape[1:], lambda i: (kind(i), 0, 0, 0)),
            pl.BlockSpec((1,) + inv.shape[1:], lambda i: (kind(i), 0, 0, 0)),
            _const_spec(w_pg.shape),
            _const_spec(w_bp.shape),
            _const_spec(w_bh.shape),
            _const_spec(w_out.shape),
            _const_spec(w_router_t.shape),
            _const_spec((n_e, 1)),
        ],
        out_specs=[
            pl.BlockSpec((tm, d), lambda i: (i, 0)),
            pl.BlockSpec((tm, d), lambda i: (i, 0)),
            pl.BlockSpec((1, 8, tm), lambda i: (i, 0, 0)),
            pl.BlockSpec((1, 8, tm), lambda i: (i, 0, 0)),
        ],
        out_shape=[
            jax.ShapeDtypeStruct((ntot, d), F32),
            jax.ShapeDtypeStruct((ntot, d), F32),
            jax.ShapeDtypeStruct((n_tiles, 8, tm), jnp.int32),
            jax.ShapeDtypeStruct((n_tiles, 8, tm), F32),
        ],
        scratch_shapes=[pltpu.VMEM((tm, hw), BF16), pltpu.VMEM((tm, hw), BF16)],
        compiler_params=_cparams(1),
    )(x, z, z, z, z, o2, mods_l, norm2_w.reshape(1, d), pool_scale.reshape(1, hw),
      hg_norm_w.reshape(1, hw), band, inv, w_pg, w_bp, w_bh, w_out, w_router_t,
      router_bias.astype(F32).reshape(n_e, 1))


def _row_gather_start(src_hbm, idx_ref, dst, sem, n_rows):
    def body(r, carry):
        tok = idx_ref[0, 0, r]
        pltpu.make_async_copy(src_hbm.at[pl.ds(tok, 1)], dst.at[pl.ds(r, 1)], sem).start()
        return carry

    lax.fori_loop(0, n_rows, body, 0, unroll=8)


def _row_gather_wait(src_hbm, dst, sem, n_rows):
    pltpu.make_async_copy(src_hbm.at[pl.ds(0, n_rows)], dst, sem).wait()


def _expert_kernel(blk_e_ref, nused_ref, idx0_ref, idxn_ref, x_hbm, w1_ref, w3_ref, w2_ref,
                   y_ref, xbuf, sem, w1c, w3c, w2c):
    i = pl.program_id(0)
    n_used = nused_ref[0]
    rows = xbuf.shape[1]
    slot = i % 2

    @pl.when(i == 0)
    def _():
        _row_gather_start(x_hbm, idx0_ref, xbuf.at[0], sem.at[0], rows)

    @pl.when(i + 1 < n_used)
    def _():
        _row_gather_start(x_hbm, idxn_ref, xbuf.at[1 - slot], sem.at[1 - slot], rows)

    @pl.when(i < n_used)
    def _():
        changed = jnp.logical_or(i == 0, blk_e_ref[i] != blk_e_ref[jnp.maximum(i - 1, 0)])

        @pl.when(changed)
        def _():
            w1c[...] = w1_ref[0, 0].astype(BF16)
            w3c[...] = w3_ref[0, 0].astype(BF16)
            w2c[...] = w2_ref[0, 0].astype(BF16)

        _row_gather_wait(x_hbm, xbuf.at[slot], sem.at[slot], rows)
        xb = xbuf[slot].astype(BF16)
        h1 = _dot(xb, w1c[...])
        h3 = _dot(xb, w3c[...])
        act = (h1 * _sigmoid(h1) * h3).astype(BF16)
        y_ref[...] = _dot(act, w2c[...])

    @pl.when(i >= n_used)
    def _():
        y_ref[...] = jnp.zeros_like(y_ref)


def _experts(h2, buf_tok, blk_e, n_used, w1, w3, w2, layer):
    ntot, d = h2.shape
    _, n_e, _, f = w1.shape
    n_blk = blk_e.shape[0]
    rows = MOE_ROWS
    idx = buf_tok.reshape(n_blk, 1, rows)
    grid_spec = pltpu.PrefetchScalarGridSpec(
        num_scalar_prefetch=2,
        grid=(n_blk,),
        in_specs=[
            pl.BlockSpec((1, 1, rows), lambda i, be, nu: (0, 0, 0), memory_space=pltpu.SMEM),
            pl.BlockSpec((1, 1, rows), lambda i, be, nu: (jnp.minimum(i + 1, n_blk - 1), 0, 0),
                         memory_space=pltpu.SMEM),
            pl.BlockSpec(memory_space=pl.ANY),
            pl.BlockSpec((1, 1, d, f), lambda i, be, nu: (layer, be[i], 0, 0)),
            pl.BlockSpec((1, 1, d, f), lambda i, be, nu: (layer, be[i], 0, 0)),
            pl.BlockSpec((1, 1, f, d), lambda i, be, nu: (layer, be[i], 0, 0)),
        ],
        out_specs=pl.BlockSpec((rows, d), lambda i, be, nu: (i, 0)),
        scratch_shapes=[
            pltpu.VMEM((2, rows, d), F32),
            pltpu.SemaphoreType.DMA((2,)),
            pltpu.VMEM((d, f), BF16),
            pltpu.VMEM((d, f), BF16),
            pltpu.VMEM((f, d), BF16),
        ],
    )
    return pl.pallas_call(
        _expert_kernel,
        name="experts",
        grid_spec=grid_spec,
        out_shape=jax.ShapeDtypeStruct((n_blk * rows, d), F32),
        compiler_params=_cparams(1),
    )(blk_e, n_used, idx, idx, h2, w1, w3, w2)


def _combine_kernel(d, tiles_per_sample, ctx_row, pos0_ref, posn_ref, y_hbm, x_ref, g_ref, mod_ref,
                    xo_ref, rbuf, sem):
    i = pl.program_id(0)
    n = pl.num_programs(0)
    tm = x_ref.shape[0]
    slot = i % 2

    @pl.when(i == 0)
    def _():
        _row_gather_start(y_hbm, pos0_ref, rbuf.at[0], sem.at[0], 2 * tm)

    @pl.when(i + 1 < n)
    def _():
        _row_gather_start(y_hbm, posn_ref, rbuf.at[1 - slot], sem.at[1 - slot], 2 * tm)

    _row_gather_wait(y_hbm, rbuf.at[slot], sem.at[slot], 2 * tm)
    mrow = jnp.where(i % tiles_per_sample == 0, ctx_row, i // tiles_per_sample)
    m5 = mod_ref[pl.ds(mrow, 1), 5 * d:6 * d]
    g = g_ref[...]
    f = g[:, 0:1] * rbuf[slot, 0:tm, :] + g[:, 1:2] * rbuf[slot, tm:2 * tm, :]
    xo_ref[...] = x_ref[...] + m5 * f


def _combine(x, yb, pos, gates, mods_l, t_sample, ctx_row):
    ntot, d = x.shape
    tm = ROW_TILE
    n_tiles = ntot // tm
    return pl.pallas_call(
        functools.partial(_combine_kernel, d, t_sample // tm, ctx_row),
        name="combine",
        grid=(n_tiles,),
        in_specs=[
            pl.BlockSpec((1, 1, 2 * tm), lambda i: (0, 0, 0), memory_space=pltpu.SMEM),
            pl.BlockSpec((1, 1, 2 * tm), lambda i: (jnp.minimum(i + 1, n_tiles - 1), 0, 0),
                         memory_space=pltpu.SMEM),
            pl.BlockSpec(memory_space=pl.ANY),
            pl.BlockSpec((tm, d), lambda i: (i, 0)),
            pl.BlockSpec((tm, 2), lambda i: (i, 0)),
            _const_spec((8, 6 * d)),
        ],
        out_specs=pl.BlockSpec((tm, d), lambda i: (i, 0)),
        out_shape=jax.ShapeDtypeStruct((ntot, d), F32),
        scratch_shapes=[pltpu.VMEM((2, 2 * tm, d), F32), pltpu.SemaphoreType.DMA((2,))],
        compiler_params=_cparams(1),
    )(pos, pos, yb, x, gates, mods_l)


def _final_kernel(x_ref, w_ref, o_ref):
    x = x_ref[...]
    o_ref[...] = x * lax.rsqrt(jnp.mean(x * x, axis=-1, keepdims=True) + EPS) * w_ref[...]


def _final_norm(x, w, n_batch, n_ctx, n_lat):
    ntot, d = x.shape
    tm = ROW_TILE
    tps = (n_ctx + n_lat) // tm
    cps = n_ctx // tm
    lps = n_lat // tm
    return pl.pallas_call(
        _final_kernel,
        name="final_norm",
        grid=(n_batch * lps,),
        in_specs=[
            pl.BlockSpec((tm, d), lambda i: ((i // lps) * tps + cps + i % lps, 0)),
            pl.BlockSpec((1, d), lambda i: (0, 0)),
        ],
        out_specs=pl.BlockSpec((tm, d), lambda i: (i, 0)),
        out_shape=jax.ShapeDtypeStruct((n_batch * n_lat, d), F32),
        compiler_params=_cparams(1),
    )(x, w.reshape(1, d))


def _dispatch_plan(eidx, gate, ntot, n_e):
    rows = MOE_ROWS
    n_tiles, _, tm = eidx.shape
    e = jnp.transpose(eidx[:, 0:2, :], (0, 2, 1)).reshape(ntot, 2)
    g = jnp.transpose(gate[:, 0:2, :], (0, 2, 1)).reshape(ntot, 2)
    flat_e = e.reshape(-1)
    n_assign = flat_e.shape[0]
    onehot = (flat_e[:, None] == jnp.arange(n_e, dtype=jnp.int32)[None, :]).astype(jnp.int32)
    counts = onehot.sum(axis=0)
    rank = jnp.sum((jnp.cumsum(onehot, axis=0) - onehot) * onehot, axis=1)
    pcounts = (counts + rows - 1) // rows * rows
    pends = jnp.cumsum(pcounts)
    dest = (pends - pcounts)[flat_e] + rank
    n_blk = -(-n_assign // rows) + n_e
    tok = jnp.arange(n_assign, dtype=jnp.int32) // 2
    buf_tok = jnp.zeros((n_blk * rows,), jnp.int32).at[dest].set(tok)
    blk_e = jnp.minimum(
        jnp.searchsorted(pends, jnp.arange(n_blk, dtype=jnp.int32) * rows, side='right'), n_e - 1
    ).astype(jnp.int32)
    n_used = (pends[-1] // rows).astype(jnp.int32).reshape(1)
    pos = jnp.transpose(dest.astype(jnp.int32).reshape(n_tiles, tm, 2), (0, 2, 1)).reshape(n_tiles, 1, 2 * tm)
    return buf_tok, blk_e, n_used, pos, g


def _inproj_rows(t_sample):
    best = 16
    for k in range(16, 1089, 16):
        if t_sample % k == 0:
            best = k
    return best


def kernel(x, c, ctx, c_ctx, w_ada, b_ada, norm1_w, norm2_w, w_in, w_pool_group, pool_scale, hg_norm_w,
           lb_logits, w_branch_pool, w_branch_hgrn, w_out, w_router, router_bias, w_e1, w_e3, w_e2,
           final_norm_w):
    n_batch, n_lat, d = x.shape
    n_ctx = ctx.shape[1]
    depth = w_ada.shape[0]
    hw = hg_norm_w.shape[1]
    n_e = w_router.shape[1]
    t_sample = n_ctx + n_lat
    ntot = n_batch * t_sample
    ctx_row = n_batch
    assert n_batch < 8 and n_ctx == ROW_TILE and n_lat % ROW_TILE == 0

    xs = jnp.concatenate([ctx, x], axis=1).reshape(ntot, d)
    cc = jnp.zeros((8, d), F32).at[:n_batch].set(c).at[ctx_row].set(c_ctx)
    mods = _ada_mods(cc, w_ada, b_ada)
    lb = _lower_bounds(lb_logits)
    w_router_t = jnp.transpose(w_router.astype(F32))
    tm_in = _inproj_rows(t_sample)

    for l in range(depth):
        z = _in_proj(xs, mods[l], norm1_w[l], w_in, l, n_ctx, t_sample, ctx_row, tm_in)
        o2 = _scan(z, lb, l, depth, n_batch, n_ctx, n_lat, hw)
        xs, h2, eidx, gate = _merge(
            xs, z, o2, mods[l], norm2_w[l], pool_scale[l], hg_norm_w[l], w_pool_group[l].astype(BF16),
            w_branch_pool[l].astype(BF16), w_branch_hgrn[l].astype(BF16), w_out[l].astype(BF16),
            w_router_t, router_bias, n_ctx, t_sample, ctx_row)
        buf_tok, blk_e, n_used, pos, g = _dispatch_plan(eidx, gate, ntot, n_e)
        yb = _experts(h2, buf_tok, blk_e, n_used, w_e1, w_e3, w_e2, l)
        xs = _combine(xs, yb, pos, g, mods[l], t_sample, ctx_row)

    out = _final_norm(xs, final_norm_w, n_batch, n_ctx, n_lat)
    return out.reshape(n_batch, n_lat, d)
```

```python
import functools
import math

import numpy as np
import jax
import jax.numpy as jnp
from jax import lax
from jax.experimental import pallas as pl
from jax.experimental.pallas import tpu as pltpu

F32 = jnp.float32
BF16 = jnp.bfloat16

EPS = 1e-6
GRID_W = 64
POOL_WINDOWS = (2, 4, 8, 16)
HEAD_DIM = 128
N_EXPERT_GROUPS = 4
EXPERTS_PER_GROUP = 4
SCAN_CHUNK = 64
ROW_TILE = 256
MOE_ROWS = 256
V7X_VMEM_BYTES = 64 * 1024 * 1024
VMEM_LIMIT = V7X_VMEM_BYTES - 8 * 1024 * 1024


def _cparams(n_axes, vmem=VMEM_LIMIT):
    return pltpu.CompilerParams(dimension_semantics=("arbitrary",) * n_axes, vmem_limit_bytes=vmem)


def _sigmoid(x):
    return 1.0 / (1.0 + jnp.exp(-x))


def _dot(a, b):
    return jnp.dot(a, b, preferred_element_type=F32)


def _dot_nt(a, b):
    return lax.dot_general(a, b, (((1,), (1,)), ((), ())), preferred_element_type=F32)


def _lb_kernel(depth, logit_ref, lb_ref):
    for d in range(2):
        rows = [logit_ref[d * depth + i:d * depth + i + 1, :] for i in range(depth)]
        m = rows[0]
        for r in rows[1:]:
            m = jnp.maximum(m, r)
        es = [jnp.exp(r - m) for r in rows]
        tot = es[0]
        for e in es[1:]:
            tot = tot + e
        ws = [e / tot for e in es]
        acc = jnp.zeros_like(ws[0])
        for i in range(depth):
            acc = acc + ws[i]
            lb_ref[d * depth + i:d * depth + i + 1, :] = acc - ws[0]


def _lower_bounds(lb_logits):
    _, depth, hw = lb_logits.shape
    flat = lb_logits.astype(F32).reshape(2 * depth, hw)
    out = pl.pallas_call(
        functools.partial(_lb_kernel, depth),
        name="lower_bounds",
        out_shape=jax.ShapeDtypeStruct((2 * depth, hw), F32),
    )(flat)
    return out.reshape(2 * depth, 1, hw)


def _ada_kernel(c_ref, w_ref, b_ref, o_ref):
    c = c_ref[...]
    s = (c * _sigmoid(c)).astype(BF16)
    o_ref[0] = _dot(s, w_ref[0].astype(BF16)) + b_ref[0]


def _ada_mods(cc, w_ada, b_ada):
    depth, d, n6 = w_ada.shape
    tn = 1024
    return pl.pallas_call(
        _ada_kernel,
        name="ada_mods",
        grid=(depth, n6 // tn),
        in_specs=[
            pl.BlockSpec((8, d), lambda l, j: (0, 0)),
            pl.BlockSpec((1, d, tn), lambda l, j: (l, 0, j)),
            pl.BlockSpec((1, 1, tn), lambda l, j: (l, 0, j)),
        ],
        out_specs=pl.BlockSpec((1, 8, tn), lambda l, j: (l, 0, j)),
        out_shape=jax.ShapeDtypeStruct((depth, 8, n6), F32),
        compiler_params=_cparams(2),
    )(cc, w_ada, b_ada.reshape(depth, 1, n6))


INPROJ_MAX_ROWS = 2176
INPROJ_COLS = 1024


def _mod_row(i, tiles_per_sample, ctx_row):
    return jnp.where(i % tiles_per_sample == 0, ctx_row, i // tiles_per_sample)


def _norm_mod(x, nw, shift, scale):
    n = x * lax.rsqrt(jnp.mean(x * x, axis=-1, keepdims=True) + EPS) * nw
    return n * (1.0 + scale) + shift


def _normmod_kernel(d, tiles_per_sample, ctx_row, x_ref, mod_ref, nw_ref, h_ref):
    mrow = _mod_row(pl.program_id(0), tiles_per_sample, ctx_row)
    shift = mod_ref[pl.ds(mrow, 1), 0:d]
    scale = mod_ref[pl.ds(mrow, 1), d:2 * d]
    h_ref[...] = _norm_mod(x_ref[...], nw_ref[...], shift, scale).astype(BF16)


def _norm_mod_rows(x, mods_l, norm_w, t_sample, ctx_row):
    ntot, d = x.shape
    tm = ROW_TILE
    return pl.pallas_call(
        functools.partial(_normmod_kernel, d, t_sample // tm, ctx_row),
        name="norm_mod",
        grid=(ntot // tm,),
        in_specs=[
            pl.BlockSpec((tm, d), lambda i: (i, 0)),
            pl.BlockSpec((8, 2 * d), lambda i: (0, 0)),
            pl.BlockSpec((1, d), lambda i: (0, 0)),
        ],
        out_specs=pl.BlockSpec((tm, d), lambda i: (i, 0)),
        out_shape=jax.ShapeDtypeStruct((ntot, d), BF16),
        compiler_params=_cparams(1),
    )(x, mods_l, norm_w.reshape(1, d))


def _inproj_kernel(h_ref, w_ref, o_ref):
    o_ref[...] = _dot(h_ref[...], w_ref[0].astype(BF16)).astype(BF16)


def _in_proj(h, w_in, layer):
    ntot, d = h.shape
    n_cols = w_in.shape[2]
    tn = INPROJ_COLS
    tm = max(k for k in range(16, INPROJ_MAX_ROWS + 1, 16) if ntot % k == 0)
    return pl.pallas_call(
        _inproj_kernel,
        name="in_proj",
        grid=(ntot // tm, n_cols // tn),
        in_specs=[
            pl.BlockSpec((tm, d), lambda r, j: (r, 0)),
            pl.BlockSpec((1, d, tn), lambda r, j: (layer, 0, j)),
        ],
        out_specs=pl.BlockSpec((tm, tn), lambda r, j: (r, j)),
        out_shape=jax.ShapeDtypeStruct((ntot, n_cols), BF16),
        compiler_params=_cparams(2),
    )(h, w_in)


COMPACT_MIN_HALF = 8
MXU_LEVEL_MIN_HALF = 2


def _scan_levels(c):
    out, m = [], c // 2
    while m >= MXU_LEVEL_MIN_HALF:
        out.append(m)
        m //= 2
    return out


def _scan_row_layout(c):
    off = {"b": 0, "kend": c, "tot": 2 * c}
    pos = 2 * c + 8
    for m in _scan_levels(c):
        n = c // 2 if m >= COMPACT_MIN_HALF else c
        off[("q", m)] = pos
        off[("k", m)] = pos + n
        pos += 2 * n
    return off, pos


def _scan_constants(c):
    t = np.arange(c)[:, None]
    r = np.arange(c)[None, :]
    out = []
    for flip in (False, True):
        fl = (lambda a: a[::-1, ::-1]) if flip else (lambda a: a)
        sets = [fl(r <= t), fl(r > t), np.ones((8, c), bool)]
        for m in _scan_levels(c):
            mid = (t // (2 * m)) * (2 * m) + m
            qa = fl((t >= mid) & (r >= mid) & (r <= t))
            ka = fl((t < mid) & (r > t) & (r <= mid - 1))
            if m >= COMPACT_MIN_HALF:
                rows = np.arange(c)
                q_valid = (rows % (2 * m) >= m) != flip
                sets += [qa[q_valid], ka[~q_valid]]
            else:
                sets += [qa, ka]
        mat = np.concatenate(sets, axis=0).astype(np.float32)
        out.append(np.concatenate([mat, mat], axis=1))
    return np.stack(out)


def _tile_pairs(e, m, c):
    parts = []
    for p in range(c // (2 * m)):
        blk = e[p * m:(p + 1) * m]
        parts += [blk, blk]
    return jnp.concatenate(parts, axis=0)


def _scan_kernel(n_heads, c, q_ref, f_ref, i_ref, lb_ref, m_ref, o_ref, s_ref, e_ref, a_ref):
    d = pl.program_id(1)
    step = pl.program_id(2)
    levels = _scan_levels(c)
    off, _ = _scan_row_layout(c)

    @pl.when(step == 0)
    def _():
        s_ref[...] = jnp.zeros_like(s_ref)

    zq = q_ref[...].astype(F32)
    zf = f_ref[...].astype(F32)
    lbv = lb_ref[0]
    f = lbv + (1.0 - lbv) * _sigmoid(zf)
    kk = 1.0 - f
    qq = zq * _sigmoid(zq)
    lf2 = jnp.log(f) * math.log2(math.e)
    lf_hi = lf2.astype(BF16)
    lf_lo = (lf2 - lf_hi.astype(F32)).astype(BF16)
    e_ref[...] = jnp.exp2(_dot(m_ref[0], jnp.concatenate([lf_hi, lf_lo], axis=0)))

    row = lax.broadcasted_iota(jnp.int32, (c, 1), 0)
    k_sw = jnp.where((row & 1) == 1, pltpu.roll(kk, 1, axis=0), pltpu.roll(kk, c - 1, axis=0))
    qf = qq * f

    ti = lax.broadcasted_iota(jnp.int32, (c, c), 0)
    si = lax.broadcasted_iota(jnp.int32, (c, c), 1)
    same = {m: (ti & -(2 * m)) == (si & -(2 * m)) for m in levels[1:] + [1]}
    eye = ti == si
    causal = (ti - si) * (1 - 2 * d) >= 0

    for h in range(n_heads):
        cs = slice(h * HEAD_DIM, (h + 1) * HEAD_DIM)
        qh = qq[:, cs]
        kh = kk[:, cs]
        a = None
        for m in levels:
            n = c // 2 if m >= COMPACT_MIN_HALF else c
            eq = e_ref[off[("q", m)]:off[("q", m)] + n, cs]
            ek = e_ref[off[("k", m)]:off[("k", m)] + n, cs]
            if m >= COMPACT_MIN_HALF:
                eq, ek = _tile_pairs(eq, m, c), _tile_pairs(ek, m, c)
            p = _dot_nt((qh * eq).astype(BF16), (kh * ek).astype(BF16))
            a = p if a is None else jnp.where(same[m], p, a)
        pair = jnp.sum(qf[:, cs] * k_sw[:, cs], axis=-1, keepdims=True)
        diag = jnp.sum(qh * kh, axis=-1, keepdims=True)
        a = jnp.where(same[1], pair, a)
        a = jnp.where(eye, diag, a)
        a_ref[h] = jnp.where(causal, a, 0.0).astype(BF16)

    for h in range(n_heads):
        cs = slice(h * HEAD_DIM, (h + 1) * HEAD_DIM)
        qh = qq[:, cs]
        kh = kk[:, cs]
        vh = i_ref[:, cs]
        st = s_ref[h]
        qb = (qh * e_ref[off["b"]:off["b"] + c, cs]).astype(BF16)
        o = _dot_nt(qb, st.astype(BF16)) + _dot(a_ref[h], vh)
        o_ref[0, :, cs] = o.astype(BF16)
        vt = vh.astype(F32).T.astype(BF16)
        kt = (kh * e_ref[off["kend"]:off["kend"] + c, cs]).astype(BF16)
        s_ref[h] = st * e_ref[off["tot"]:off["tot"] + 1, cs] + _dot(vt, kt)


def _scan(z, lb, layer, depth, n_batch, n_ctx, n_lat, hw):
    c = SCAN_CHUNK
    n_heads = hw // HEAD_DIM
    nc, nl = n_ctx // c, n_lat // c
    steps = nc + nl
    mats = jnp.asarray(_scan_constants(c), BF16)
    n_rows = mats.shape[1]
    assert n_rows == _scan_row_layout(c)[1]
    ntot = z.shape[0]

    def chunk(b, d, s):
        bwd = jnp.where(s < nc, nc - 1 - s, nc + (steps - 1 - s))
        return b * steps + jnp.where(d == 0, s, bwd)

    return pl.pallas_call(
        functools.partial(_scan_kernel, n_heads, c),
        name="hgrn2_scan",
        grid=(n_batch, 2, steps),
        in_specs=[
            pl.BlockSpec((c, hw), lambda b, d, s: (chunk(b, d, s), 1)),
            pl.BlockSpec((c, hw), lambda b, d, s: (chunk(b, d, s), 2 + d)),
            pl.BlockSpec((c, hw), lambda b, d, s: (chunk(b, d, s), 4)),
            pl.BlockSpec((1, 1, hw), lambda b, d, s: (d * depth + layer, 0, 0)),
            pl.BlockSpec((1, n_rows, 2 * c), lambda b, d, s: (d, 0, 0)),
        ],
        out_specs=pl.BlockSpec((1, c, hw), lambda b, d, s: (d, chunk(b, d, s), 0)),
        out_shape=jax.ShapeDtypeStruct((2, ntot, hw), BF16),
        scratch_shapes=[pltpu.VMEM((n_heads, HEAD_DIM, HEAD_DIM), F32), pltpu.VMEM((n_rows, hw), F32),
                        pltpu.VMEM((n_heads, c, c), BF16)],
        compiler_params=_cparams(3),
    )(z, z, z, lb, mats)


def _pool_constants(tile, n_ctx):
    assert n_ctx == tile and tile % GRID_W == 0
    band = np.zeros((2, len(POOL_WINDOWS), tile, tile), np.float32)
    inv = np.zeros((2, len(POOL_WINDOWS), tile, 1), np.float32)
    for kind, row_len in enumerate((n_ctx, GRID_W)):
        t = np.arange(tile)
        tau = t % row_len
        base = t - tau
        for g, w in enumerate(POOL_WINDOWS):
            lo = np.maximum(tau - w // 2, 0)
            hi = np.minimum(tau + w // 2 - 1, row_len - 1)
            s = np.arange(tile)[None, :]
            band[kind, g] = (s >= (base + lo)[:, None]) & (s <= (base + hi)[:, None])
            inv[kind, g, :, 0] = 1.0 / (hi - lo + 1)
    return band, inv


def _route(sel, sc):
    n_g, per = N_EXPERT_GROUPS, EXPERTS_PER_GROUP
    row = lambda a, e: a[e:e + 1, :]
    best = None
    for g in range(n_g):
        rows = [row(sel, g * per + i) for i in range(per)]
        gs = None
        for i in range(per):
            for j in range(i + 1, per):
                p = rows[i] + rows[j]
                gs = p if gs is None else jnp.maximum(gs, p)
        if best is None:
            best, g_idx = gs, jnp.zeros(gs.shape, jnp.int32)
        else:
            upd = gs > best
            g_idx = jnp.where(upd, g, g_idx)
            best = jnp.where(upd, gs, best)
    cand, raw = [], []
    for i in range(per):
        ci, ri = row(sel, i), row(sc, i)
        for g in range(1, n_g):
            ci = jnp.where(g_idx == g, row(sel, g * per + i), ci)
            ri = jnp.where(g_idx == g, row(sc, g * per + i), ri)
        cand.append(ci)
        raw.append(ri)

    def argmax4(vals):
        m, loc = vals[0], jnp.zeros(vals[0].shape, jnp.int32)
        for i in range(1, per):
            upd = vals[i] > m
            loc = jnp.where(upd, i, loc)
            m = jnp.where(upd, vals[i], m)
        return loc

    loc0 = argmax4(cand)
    loc1 = argmax4([jnp.where(loc0 == i, -jnp.inf, cand[i]) for i in range(per)])

    def pick(loc):
        out = raw[0]
        for i in range(1, per):
            out = jnp.where(loc == i, raw[i], out)
        return out

    g0, g1 = pick(loc0), pick(loc1)
    tot = g0 + g1
    return g_idx * per + loc0, g_idx * per + loc1, g0 / tot, g1 / tot


def _merge_kernel(d, hw, tiles_per_sample, ctx_row,
                  x_ref, p_ref, og_ref, gp_ref, gh_ref, o_ref, mod_ref, n2_ref, ps_ref, hn_ref,
                  band_ref, inv_ref, wpg_ref, wbp_ref, wbh_ref, wout_ref, wr_ref, rb_ref, tri_ref,
                  xo_ref, h2_ref, eidx_ref, gate_ref, cnt_ref, pool_scr, hg_scr, carry_ref):
    i = pl.program_id(0)
    n_heads = hw // HEAD_DIM
    n_groups = len(POOL_WINDOWS)
    pg = p_ref.shape[1] // n_groups
    mrow = _mod_row(i, tiles_per_sample, ctx_row)
    mod = lambda k: mod_ref[pl.ds(mrow, 1), k * d:(k + 1) * d]

    for h in range(n_heads):
        cs = slice(h * HEAD_DIM, (h + 1) * HEAD_DIM)
        o = o_ref[0, :, cs].astype(F32) + o_ref[1, :, cs].astype(F32)
        o = o * lax.rsqrt(jnp.mean(o * o, axis=-1, keepdims=True) + EPS)
        og = og_ref[:, cs].astype(F32)
        hg_scr[:, cs] = (o * hn_ref[:, cs] * (og * _sigmoid(og))).astype(BF16)

    for g in range(n_groups):
        cs = slice(g * pg, (g + 1) * pg)
        u = p_ref[:, cs]
        wsum = _dot(band_ref[0, g], u)
        dlt = wsum * inv_ref[0, g] - u.astype(F32)
        pool_scr[:, cs] = (_dot(dlt.astype(BF16), wpg_ref[g]) * ps_ref[:, cs]).astype(BF16)

    gp = gp_ref[...].astype(F32)
    gh = gh_ref[...].astype(F32)
    m = _sigmoid(gp) * _dot(pool_scr[...], wbp_ref[...]) + _sigmoid(gh) * _dot(hg_scr[...], wbh_ref[...])
    y = _dot(m.astype(BF16), wout_ref[...])
    xn = x_ref[...] + mod(2) * y
    xo_ref[...] = xn
    n2 = xn * lax.rsqrt(jnp.mean(xn * xn, axis=-1, keepdims=True) + EPS) * n2_ref[...]
    h2 = n2 * (1.0 + mod(4)) + mod(3)
    h2_ref[...] = h2

    h_hi = h2.astype(BF16)
    h_lo = (h2 - h_hi.astype(F32)).astype(BF16)
    wr = wr_ref[...]
    w_hi = wr.astype(BF16)
    w_lo = (wr - w_hi.astype(F32)).astype(BF16)
    logits = _dot_nt(w_hi, h_hi) + (_dot_nt(w_hi, h_lo) + _dot_nt(w_lo, h_hi))
    sc = _sigmoid(logits)
    e0, e1, g0, g1 = _route(sc + rb_ref[...], sc)

    @pl.when(i == 0)
    def _():
        carry_ref[...] = jnp.zeros_like(carry_ref)

    er = lax.broadcasted_iota(jnp.int32, sc.shape, 0)
    oh0 = (er == e0).astype(F32)
    oh1 = (er == e1).astype(F32)
    both = oh0 + oh1
    base = carry_ref[:, 0:1] + _dot(both.astype(BF16), tri_ref[...])
    r0 = jnp.sum(oh0 * base, axis=0, keepdims=True)
    r1 = jnp.sum(oh1 * base, axis=0, keepdims=True)
    carry_ref[...] = carry_ref[...] + jnp.sum(both, axis=1, keepdims=True)
    cnt_ref[...] = carry_ref[...].astype(jnp.int32)

    eidx_ref[...] = jnp.zeros_like(eidx_ref)
    gate_ref[...] = jnp.zeros_like(gate_ref)
    eidx_ref[0, 0:1, :] = e0
    eidx_ref[0, 1:2, :] = e1
    eidx_ref[0, 2:3, :] = r0.astype(jnp.int32)
    eidx_ref[0, 3:4, :] = r1.astype(jnp.int32)
    gate_ref[0, 0:1, :] = g0
    gate_ref[0, 1:2, :] = g1


def _const_spec(shape):
    nd = len(shape)
    return pl.BlockSpec(shape, lambda i: (0,) * nd, pipeline_mode=pl.Buffered(1))


def _merge(x, z, o2, mods_l, norm2_w, pool_scale, hg_norm_w, w_pg, w_bp, w_bh, w_out, w_router_t,
           router_bias, n_ctx, t_sample, ctx_row):
    ntot, d = x.shape
    hw = o2.shape[2]
    d_pool = w_bp.shape[0]
    assert d_pool == hw and d == 2 * hw
    tm = ROW_TILE
    tps = t_sample // tm
    n_e = w_router_t.shape[0]
    band, inv = _pool_constants(tm, n_ctx)
    band = jnp.asarray(band, BF16)
    inv = jnp.asarray(inv, F32)
    kind = lambda i: jnp.where(i % tps == 0, 0, 1)
    n_tiles = ntot // tm
    before = jnp.asarray(np.triu(np.ones((tm, tm), np.float32), 1), BF16)
    return pl.pallas_call(
        functools.partial(_merge_kernel, d, hw, tps, ctx_row),
        name="merge",
        grid=(n_tiles,),
        in_specs=[
            pl.BlockSpec((tm, d), lambda i: (i, 0)),
            pl.BlockSpec((tm, hw), lambda i: (i, 0)),
            pl.BlockSpec((tm, hw), lambda i: (i, 5)),
            pl.BlockSpec((tm, d), lambda i: (i, 3)),
            pl.BlockSpec((tm, d), lambda i: (i, 4)),
            pl.BlockSpec((2, tm, hw), lambda i: (0, i, 0)),
            _const_spec((8, 6 * d)),
            _const_spec((1, d)),
            _const_spec((1, hw)),
            _const_spec((1, hw)),
            pl.BlockSpec((1,) + band.shape[1:], lambda i: (kind(i), 0, 0, 0)),
            pl.BlockSpec((1,) + inv.shape[1:], lambda i: (kind(i), 0, 0, 0)),
            _const_spec(w_pg.shape),
            _const_spec(w_bp.shape),
            _const_spec(w_bh.shape),
            _const_spec(w_out.shape),
            _const_spec(w_router_t.shape),
            _const_spec((n_e, 1)),
            _const_spec((tm, tm)),
        ],
        out_specs=[
            pl.BlockSpec((tm, d), lambda i: (i, 0)),
            pl.BlockSpec((tm, d), lambda i: (i, 0)),
            pl.BlockSpec((1, 8, tm), lambda i: (i, 0, 0)),
            pl.BlockSpec((1, 8, tm), lambda i: (i, 0, 0)),
            pl.BlockSpec((n_e, 128), lambda i: (0, 0)),
        ],
        out_shape=[
            jax.ShapeDtypeStruct((ntot, d), F32),
            jax.ShapeDtypeStruct((ntot, d), F32),
            jax.ShapeDtypeStruct((n_tiles, 8, tm), jnp.int32),
            jax.ShapeDtypeStruct((n_tiles, 8, tm), F32),
            jax.ShapeDtypeStruct((n_e, 128), jnp.int32),
        ],
        scratch_shapes=[pltpu.VMEM((tm, hw), BF16), pltpu.VMEM((tm, hw), BF16), pltpu.VMEM((n_e, 128), F32)],
        compiler_params=_cparams(1),
    )(x, z, z, z, z, o2, mods_l, norm2_w.reshape(1, d), pool_scale.reshape(1, hw),
      hg_norm_w.reshape(1, hw), band, inv, w_pg, w_bp, w_bh, w_out, w_router_t,
      router_bias.astype(F32).reshape(n_e, 1), before)


def _row_gather_start(src_hbm, idx_ref, dst, sem, n_rows):
    def body(r, carry):
        tok = idx_ref[0, 0, r]
        pltpu.make_async_copy(src_hbm.at[pl.ds(tok, 1)], dst.at[pl.ds(r, 1)], sem).start()
        return carry

    lax.fori_loop(0, n_rows, body, 0, unroll=8)


def _row_gather_wait(src_hbm, dst, sem, n_rows):
    pltpu.make_async_copy(src_hbm.at[pl.ds(0, n_rows)], dst, sem).wait()


def _dispatch_kernel(pos_ref, h_ref, zero_hbm, xs_hbm, sems):
    del zero_hbm
    tm = h_ref.shape[0]
    sem = sems.at[0]

    def body(r, carry):
        for k in range(2):
            dst = pos_ref[0, 0, k * tm + r]
            pltpu.make_async_copy(h_ref.at[pl.ds(r, 1)], xs_hbm.at[pl.ds(dst, 1)], sem).start()
        return carry

    lax.fori_loop(0, tm, body, 0, unroll=8)
    pltpu.make_async_copy(xs_hbm.at[pl.ds(0, 2 * tm)], xs_hbm.at[pl.ds(0, 2 * tm)], sem).wait()


def _dispatch(h2, pos, n_blk):
    ntot, d = h2.shape
    tm = ROW_TILE
    n_tiles = ntot // tm
    zeros = jnp.zeros((n_blk * MOE_ROWS, d), F32)
    return pl.pallas_call(
        _dispatch_kernel,
        name="dispatch",
        grid=(n_tiles,),
        in_specs=[
            pl.BlockSpec((1, 1, 2 * tm), lambda i: (i, 0, 0), memory_space=pltpu.SMEM),
            pl.BlockSpec((tm, d), lambda i: (i, 0)),
            pl.BlockSpec(memory_space=pl.ANY),
        ],
        out_specs=pl.BlockSpec(memory_space=pl.ANY),
        out_shape=jax.ShapeDtypeStruct((n_blk * MOE_ROWS, d), F32),
        scratch_shapes=[pltpu.SemaphoreType.DMA((1,))],
        input_output_aliases={2: 0},
        compiler_params=_cparams(1),
    )(pos, h2, zeros)


def _expert_kernel(blk_e_ref, nused_ref, x_ref, w1_ref, w3_ref, w2_ref, y_ref, w1c, w3c, w2c):
    i = pl.program_id(0)
    n_used = nused_ref[0]

    @pl.when(i < n_used)
    def _():
        changed = jnp.logical_or(i == 0, blk_e_ref[i] != blk_e_ref[jnp.maximum(i - 1, 0)])

        @pl.when(changed)
        def _():
            w1c[...] = w1_ref[0, 0].astype(BF16)
            w3c[...] = w3_ref[0, 0].astype(BF16)
            w2c[...] = w2_ref[0, 0].astype(BF16)

        xb = x_ref[...].astype(BF16)
        h1 = _dot(xb, w1c[...])
        h3 = _dot(xb, w3c[...])
        act = (h1 * _sigmoid(h1) * h3).astype(BF16)
        y_ref[...] = _dot(act, w2c[...])

    @pl.when(i >= n_used)
    def _():
        y_ref[...] = jnp.zeros_like(y_ref)


def _experts(xs, blk_e, n_used, w1, w3, w2, layer):
    n_rows, d = xs.shape
    _, n_e, _, f = w1.shape
    n_blk = blk_e.shape[0]
    rows = MOE_ROWS
    used = lambda i, nu: jnp.minimum(i, nu[0] - 1)
    grid_spec = pltpu.PrefetchScalarGridSpec(
        num_scalar_prefetch=2,
        grid=(n_blk,),
        in_specs=[
            pl.BlockSpec((rows, d), lambda i, be, nu: (used(i, nu), 0)),
            pl.BlockSpec((1, 1, d, f), lambda i, be, nu: (layer, be[i], 0, 0)),
            pl.BlockSpec((1, 1, d, f), lambda i, be, nu: (layer, be[i], 0, 0)),
            pl.BlockSpec((1, 1, f, d), lambda i, be, nu: (layer, be[i], 0, 0)),
        ],
        out_specs=pl.BlockSpec((rows, d), lambda i, be, nu: (i, 0)),
        scratch_shapes=[
            pltpu.VMEM((d, f), BF16),
            pltpu.VMEM((d, f), BF16),
            pltpu.VMEM((f, d), BF16),
        ],
    )
    return pl.pallas_call(
        _expert_kernel,
        name="experts",
        grid_spec=grid_spec,
        out_shape=jax.ShapeDtypeStruct((n_rows, d), F32),
        compiler_params=_cparams(1),
    )(blk_e, n_used, xs, w1, w3, w2)


def _combine_kernel(d, tiles_per_sample, ctx_row, emit_h, pos0_ref, posn_ref, y_hbm, x_ref, g_ref, mod_ref,
                    *rest):
    if emit_h:
        modn_ref, nw_ref, xo_ref, h_ref, rbuf, sem = rest
    else:
        xo_ref, rbuf, sem = rest
    i = pl.program_id(0)
    n = pl.num_programs(0)
    tm = x_ref.shape[0]
    slot = i % 2

    @pl.when(i == 0)
    def _():
        _row_gather_start(y_hbm, pos0_ref, rbuf.at[0], sem.at[0], 2 * tm)

    @pl.when(i + 1 < n)
    def _():
        _row_gather_start(y_hbm, posn_ref, rbuf.at[1 - slot], sem.at[1 - slot], 2 * tm)

    _row_gather_wait(y_hbm, rbuf.at[slot], sem.at[slot], 2 * tm)
    mrow = _mod_row(i, tiles_per_sample, ctx_row)
    m5 = mod_ref[pl.ds(mrow, 1), 5 * d:6 * d]
    g = g_ref[...]
    f = g[:, 0:1] * rbuf[slot, 0:tm, :] + g[:, 1:2] * rbuf[slot, tm:2 * tm, :]
    xn = x_ref[...] + m5 * f
    xo_ref[...] = xn
    if emit_h:
        shift = modn_ref[pl.ds(mrow, 1), 0:d]
        scale = modn_ref[pl.ds(mrow, 1), d:2 * d]
        h_ref[...] = _norm_mod(xn, nw_ref[...], shift, scale).astype(BF16)


def _combine(x, yb, pos, gates, mods_l, t_sample, ctx_row, mods_next=None, norm_next=None):
    ntot, d = x.shape
    tm = ROW_TILE
    n_tiles = ntot // tm
    emit_h = mods_next is not None
    in_specs = [
        pl.BlockSpec((1, 1, 2 * tm), lambda i: (0, 0, 0), memory_space=pltpu.SMEM),
        pl.BlockSpec((1, 1, 2 * tm), lambda i: (jnp.minimum(i + 1, n_tiles - 1), 0, 0),
                     memory_space=pltpu.SMEM),
        pl.BlockSpec(memory_space=pl.ANY),
        pl.BlockSpec((tm, d), lambda i: (i, 0)),
        pl.BlockSpec((tm, 2), lambda i: (i, 0)),
        _const_spec((8, 6 * d)),
    ]
    args = [pos, pos, yb, x, gates, mods_l]
    out_specs = [pl.BlockSpec((tm, d), lambda i: (i, 0))]
    out_shape = [jax.ShapeDtypeStruct((ntot, d), F32)]
    if emit_h:
        in_specs += [pl.BlockSpec((8, 2 * d), lambda i: (0, 0)), pl.BlockSpec((1, d), lambda i: (0, 0))]
        args += [mods_next, norm_next.reshape(1, d)]
        out_specs.append(pl.BlockSpec((tm, d), lambda i: (i, 0)))
        out_shape.append(jax.ShapeDtypeStruct((ntot, d), BF16))
    return pl.pallas_call(
        functools.partial(_combine_kernel, d, t_sample // tm, ctx_row, emit_h),
        name="combine",
        grid=(n_tiles,),
        in_specs=in_specs,
        out_specs=out_specs,
        out_shape=out_shape,
        scratch_shapes=[pltpu.VMEM((2, 2 * tm, d), F32), pltpu.SemaphoreType.DMA((2,))],
        compiler_params=_cparams(1),
    )(*args)


def _final_kernel(x_ref, w_ref, o_ref):
    x = x_ref[...]
    o_ref[...] = x * lax.rsqrt(jnp.mean(x * x, axis=-1, keepdims=True) + EPS) * w_ref[...]


def _final_norm(x, w, n_batch, n_ctx, n_lat):
    ntot, d = x.shape
    tm = ROW_TILE
    tps = (n_ctx + n_lat) // tm
    cps = n_ctx // tm
    lps = n_lat // tm
    return pl.pallas_call(
        _final_kernel,
        name="final_norm",
        grid=(n_batch * lps,),
        in_specs=[
            pl.BlockSpec((tm, d), lambda i: ((i // lps) * tps + cps + i % lps, 0)),
            pl.BlockSpec((1, d), lambda i: (0, 0)),
        ],
        out_specs=pl.BlockSpec((tm, d), lambda i: (i, 0)),
        out_shape=jax.ShapeDtypeStruct((n_batch * n_lat, d), F32),
        compiler_params=_cparams(1),
    )(x, w.reshape(1, d))


def _dispatch_plan(eidx, counts, n_e):
    rows = MOE_ROWS
    n_tiles, _, tm = eidx.shape
    n_assign = n_tiles * tm * 2
    pcounts = (counts + rows - 1) // rows * rows
    pends = jnp.cumsum(pcounts)
    pstart = pends - pcounts
    dest = jnp.take(pstart, eidx[:, 0:2, :], axis=0) + eidx[:, 2:4, :]
    n_blk = -(-n_assign // rows) + n_e
    blk_e = jnp.minimum(
        jnp.searchsorted(pends, jnp.arange(n_blk, dtype=jnp.int32) * rows, side='right'), n_e - 1
    ).astype(jnp.int32)
    n_used = (pends[-1] // rows).astype(jnp.int32).reshape(1)
    pos = dest.astype(jnp.int32).reshape(n_tiles, 1, 2 * tm)
    return blk_e, n_used, pos, n_blk


def kernel(x, c, ctx, c_ctx, w_ada, b_ada, norm1_w, norm2_w, w_in, w_pool_group, pool_scale, hg_norm_w,
           lb_logits, w_branch_pool, w_branch_hgrn, w_out, w_router, router_bias, w_e1, w_e3, w_e2,
           final_norm_w):
    n_batch, n_lat, d = x.shape
    n_ctx = ctx.shape[1]
    depth = w_ada.shape[0]
    hw = hg_norm_w.shape[1]
    n_e = w_router.shape[1]
    t_sample = n_ctx + n_lat
    ntot = n_batch * t_sample
    ctx_row = n_batch
    assert n_batch < 8 and n_ctx == ROW_TILE and n_lat % ROW_TILE == 0

    xs = jnp.concatenate([ctx, x], axis=1).reshape(ntot, d)
    cc = jnp.zeros((8, d), F32).at[:n_batch].set(c).at[ctx_row].set(c_ctx)
    mods = _ada_mods(cc, w_ada, b_ada)
    lb = _lower_bounds(lb_logits)
    w_router_t = jnp.transpose(w_router.astype(F32))

    h = _norm_mod_rows(xs, mods[0], norm1_w[0], t_sample, ctx_row)
    for l in range(depth):
        z = _in_proj(h, w_in, l)
        o2 = _scan(z, lb, l, depth, n_batch, n_ctx, n_lat, hw)
        xs, h2, eidx, gate, counts = _merge(
            xs, z, o2, mods[l], norm2_w[l], pool_scale[l], hg_norm_w[l], w_pool_group[l].astype(BF16),
            w_branch_pool[l].astype(BF16), w_branch_hgrn[l].astype(BF16), w_out[l].astype(BF16),
            w_router_t, router_bias, n_ctx, t_sample, ctx_row)
        blk_e, n_used, pos, n_blk = _dispatch_plan(eidx, counts[:, 0], n_e)
        yb = _experts(_dispatch(h2, pos, n_blk), blk_e, n_used, w_e1, w_e3, w_e2, l)
        g = jnp.transpose(gate[:, 0:2, :], (0, 2, 1)).reshape(ntot, 2)
        if l + 1 < depth:
            xs, h = _combine(xs, yb, pos, g, mods[l], t_sample, ctx_row, mods[l + 1], norm1_w[l + 1])
        else:
            xs, = _combine(xs, yb, pos, g, mods[l], t_sample, ctx_row)

    out = _final_norm(xs, final_norm_w, n_batch, n_ctx, n_lat)
    return out.reshape(n_batch, n_lat, d)
```

```python
import functools
import math

import numpy as np
import jax
import jax.numpy as jnp
from jax import lax
from jax.experimental import pallas as pl
from jax.experimental.pallas import tpu as pltpu

F32 = jnp.float32
BF16 = jnp.bfloat16

EPS = 1e-6
GRID_W = 64
POOL_WINDOWS = (2, 4, 8, 16)
HEAD_DIM = 128
N_EXPERT_GROUPS = 4
EXPERTS_PER_GROUP = 4
SCAN_CHUNK = 64
ROW_TILE = 256
MOE_ROWS = 256
V7X_VMEM_BYTES = 64 * 1024 * 1024
VMEM_LIMIT = V7X_VMEM_BYTES - 8 * 1024 * 1024


def _cparams(n_axes, vmem=VMEM_LIMIT):
    return pltpu.CompilerParams(dimension_semantics=("arbitrary",) * n_axes, vmem_limit_bytes=vmem)


def _sigmoid(x):
    return 1.0 / (1.0 + jnp.exp(-x))


def _dot(a, b):
    return jnp.dot(a, b, preferred_element_type=F32)


def _dot_nt(a, b):
    return lax.dot_general(a, b, (((1,), (1,)), ((), ())), preferred_element_type=F32)


def _lb_kernel(depth, logit_ref, lb_ref):
    for d in range(2):
        rows = [logit_ref[d * depth + i:d * depth + i + 1, :] for i in range(depth)]
        m = rows[0]
        for r in rows[1:]:
            m = jnp.maximum(m, r)
        es = [jnp.exp(r - m) for r in rows]
        tot = es[0]
        for e in es[1:]:
            tot = tot + e
        ws = [e / tot for e in es]
        acc = jnp.zeros_like(ws[0])
        for i in range(depth):
            acc = acc + ws[i]
            lb_ref[d * depth + i:d * depth + i + 1, :] = acc - ws[0]


def _lower_bounds(lb_logits):
    _, depth, hw = lb_logits.shape
    flat = lb_logits.astype(F32).reshape(2 * depth, hw)
    out = pl.pallas_call(
        functools.partial(_lb_kernel, depth),
        name="lower_bounds",
        out_shape=jax.ShapeDtypeStruct((2 * depth, hw), F32),
    )(flat)
    return out.reshape(2 * depth, 1, hw)


def _ada_kernel(c_ref, w_ref, b_ref, o_ref):
    c = c_ref[...]
    s = (c * _sigmoid(c)).astype(BF16)
    o_ref[0] = _dot(s, w_ref[0].astype(BF16)) + b_ref[0]


def _ada_mods(cc, w_ada, b_ada):
    depth, d, n6 = w_ada.shape
    tn = 1024
    return pl.pallas_call(
        _ada_kernel,
        name="ada_mods",
        grid=(depth, n6 // tn),
        in_specs=[
            pl.BlockSpec((8, d), lambda l, j: (0, 0)),
            pl.BlockSpec((1, d, tn), lambda l, j: (l, 0, j)),
            pl.BlockSpec((1, 1, tn), lambda l, j: (l, 0, j)),
        ],
        out_specs=pl.BlockSpec((1, 8, tn), lambda l, j: (l, 0, j)),
        out_shape=jax.ShapeDtypeStruct((depth, 8, n6), F32),
        compiler_params=_cparams(2),
    )(cc, w_ada, b_ada.reshape(depth, 1, n6))


INPROJ_MAX_ROWS = 2176
INPROJ_COLS = 1024


def _mod_row(i, tiles_per_sample, ctx_row):
    return jnp.where(i % tiles_per_sample == 0, ctx_row, i // tiles_per_sample)


def _norm_mod(x, nw, shift, scale):
    n = x * lax.rsqrt(jnp.mean(x * x, axis=-1, keepdims=True) + EPS) * nw
    return n * (1.0 + scale) + shift


def _normmod_kernel(d, tiles_per_sample, ctx_row, x_ref, mod_ref, nw_ref, h_ref):
    mrow = _mod_row(pl.program_id(0), tiles_per_sample, ctx_row)
    shift = mod_ref[pl.ds(mrow, 1), 0:d]
    scale = mod_ref[pl.ds(mrow, 1), d:2 * d]
    h_ref[...] = _norm_mod(x_ref[...], nw_ref[...], shift, scale).astype(BF16)


def _norm_mod_rows(x, mods_l, norm_w, t_sample, ctx_row):
    ntot, d = x.shape
    tm = ROW_TILE
    return pl.pallas_call(
        functools.partial(_normmod_kernel, d, t_sample // tm, ctx_row),
        name="norm_mod",
        grid=(ntot // tm,),
        in_specs=[
            pl.BlockSpec((tm, d), lambda i: (i, 0)),
            pl.BlockSpec((8, 2 * d), lambda i: (0, 0)),
            pl.BlockSpec((1, d), lambda i: (0, 0)),
        ],
        out_specs=pl.BlockSpec((tm, d), lambda i: (i, 0)),
        out_shape=jax.ShapeDtypeStruct((ntot, d), BF16),
        compiler_params=_cparams(1),
    )(x, mods_l, norm_w.reshape(1, d))


def _inproj_kernel(h_ref, w_ref, o_ref):
    o_ref[...] = _dot(h_ref[...], w_ref[0].astype(BF16)).astype(BF16)


def _in_proj(h, w_in, layer):
    ntot, d = h.shape
    n_cols = w_in.shape[2]
    tn = INPROJ_COLS
    tm = max(k for k in range(16, INPROJ_MAX_ROWS + 1, 16) if ntot % k == 0)
    return pl.pallas_call(
        _inproj_kernel,
        name="in_proj",
        grid=(ntot // tm, n_cols // tn),
        in_specs=[
            pl.BlockSpec((tm, d), lambda r, j: (r, 0)),
            pl.BlockSpec((1, d, tn), lambda r, j: (layer, 0, j)),
        ],
        out_specs=pl.BlockSpec((tm, tn), lambda r, j: (r, j)),
        out_shape=jax.ShapeDtypeStruct((ntot, n_cols), BF16),
        compiler_params=_cparams(2),
    )(h, w_in)


COMPACT_MIN_HALF = 8
MXU_LEVEL_MIN_HALF = 2


def _scan_levels(c):
    out, m = [], c // 2
    while m >= MXU_LEVEL_MIN_HALF:
        out.append(m)
        m //= 2
    return out


def _scan_row_layout(c):
    off = {"b": 0, "kend": c, "tot": 2 * c}
    pos = 2 * c + 8
    for m in _scan_levels(c):
        n = c // 2 if m >= COMPACT_MIN_HALF else c
        off[("q", m)] = pos
        off[("k", m)] = pos + n
        pos += 2 * n
    return off, pos


def _scan_constants(c):
    t = np.arange(c)[:, None]
    r = np.arange(c)[None, :]
    out = []
    for flip in (False, True):
        fl = (lambda a: a[::-1, ::-1]) if flip else (lambda a: a)
        sets = [fl(r <= t), fl(r > t), np.ones((8, c), bool)]
        for m in _scan_levels(c):
            mid = (t // (2 * m)) * (2 * m) + m
            qa = fl((t >= mid) & (r >= mid) & (r <= t))
            ka = fl((t < mid) & (r > t) & (r <= mid - 1))
            if m >= COMPACT_MIN_HALF:
                rows = np.arange(c)
                q_valid = (rows % (2 * m) >= m) != flip
                sets += [qa[q_valid], ka[~q_valid]]
            else:
                sets += [qa, ka]
        mat = np.concatenate(sets, axis=0).astype(np.float32)
        out.append(np.concatenate([mat, mat], axis=1))
    return np.stack(out)


def _tile_pairs(e, m, c):
    parts = []
    for p in range(c // (2 * m)):
        blk = e[p * m:(p + 1) * m]
        parts += [blk, blk]
    return jnp.concatenate(parts, axis=0)


def _scan_kernel(n_heads, c, q_ref, f_ref, i_ref, lb_ref, m_ref, o_ref, s_ref, e_ref, a_ref):
    d = pl.program_id(1)
    step = pl.program_id(2)
    levels = _scan_levels(c)
    off, _ = _scan_row_layout(c)

    @pl.when(step == 0)
    def _():
        s_ref[...] = jnp.zeros_like(s_ref)

    zq = q_ref[...].astype(F32)
    zf = f_ref[...].astype(F32)
    lbv = lb_ref[0]
    f = lbv + (1.0 - lbv) * _sigmoid(zf)
    kk = 1.0 - f
    qq = zq * _sigmoid(zq)
    lf2 = jnp.log(f) * math.log2(math.e)
    lf_hi = lf2.astype(BF16)
    lf_lo = (lf2 - lf_hi.astype(F32)).astype(BF16)
    e_ref[...] = jnp.exp2(_dot(m_ref[0], jnp.concatenate([lf_hi, lf_lo], axis=0)))

    row = lax.broadcasted_iota(jnp.int32, (c, 1), 0)
    k_sw = jnp.where((row & 1) == 1, pltpu.roll(kk, 1, axis=0), pltpu.roll(kk, c - 1, axis=0))
    qf = qq * f

    ti = lax.broadcasted_iota(jnp.int32, (c, c), 0)
    si = lax.broadcasted_iota(jnp.int32, (c, c), 1)
    same = {m: (ti & -(2 * m)) == (si & -(2 * m)) for m in levels[1:] + [1]}
    eye = ti == si
    causal = (ti - si) * (1 - 2 * d) >= 0

    for h in range(n_heads):
        cs = slice(h * HEAD_DIM, (h + 1) * HEAD_DIM)
        qh = qq[:, cs]
        kh = kk[:, cs]
        a = None
        for m in levels:
            n = c // 2 if m >= COMPACT_MIN_HALF else c
            eq = e_ref[off[("q", m)]:off[("q", m)] + n, cs]
            ek = e_ref[off[("k", m)]:off[("k", m)] + n, cs]
            if m >= COMPACT_MIN_HALF:
                eq, ek = _tile_pairs(eq, m, c), _tile_pairs(ek, m, c)
            p = _dot_nt((qh * eq).astype(BF16), (kh * ek).astype(BF16))
            a = p if a is None else jnp.where(same[m], p, a)
        pair = jnp.sum(qf[:, cs] * k_sw[:, cs], axis=-1, keepdims=True)
        diag = jnp.sum(qh * kh, axis=-1, keepdims=True)
        a = jnp.where(same[1], pair, a)
        a = jnp.where(eye, diag, a)
        a_ref[h] = jnp.where(causal, a, 0.0).astype(BF16)

    for h in range(n_heads):
        cs = slice(h * HEAD_DIM, (h + 1) * HEAD_DIM)
        qh = qq[:, cs]
        kh = kk[:, cs]
        vh = i_ref[:, cs]
        st = s_ref[h]
        qb = (qh * e_ref[off["b"]:off["b"] + c, cs]).astype(BF16)
        o = _dot_nt(qb, st.astype(BF16)) + _dot(a_ref[h], vh)
        o_ref[0, :, cs] = o.astype(BF16)
        vt = vh.astype(F32).T.astype(BF16)
        kt = (kh * e_ref[off["kend"]:off["kend"] + c, cs]).astype(BF16)
        s_ref[h] = st * e_ref[off["tot"]:off["tot"] + 1, cs] + _dot(vt, kt)


def _scan(z, lb, layer, depth, n_batch, n_ctx, n_lat, hw):
    c = SCAN_CHUNK
    n_heads = hw // HEAD_DIM
    nc, nl = n_ctx // c, n_lat // c
    steps = nc + nl
    mats = jnp.asarray(_scan_constants(c), BF16)
    n_rows = mats.shape[1]
    assert n_rows == _scan_row_layout(c)[1]
    ntot = z.shape[0]

    def chunk(b, d, s):
        bwd = jnp.where(s < nc, nc - 1 - s, nc + (steps - 1 - s))
        return b * steps + jnp.where(d == 0, s, bwd)

    return pl.pallas_call(
        functools.partial(_scan_kernel, n_heads, c),
        name="hgrn2_scan",
        grid=(n_batch, 2, steps),
        in_specs=[
            pl.BlockSpec((c, hw), lambda b, d, s: (chunk(b, d, s), 1)),
            pl.BlockSpec((c, hw), lambda b, d, s: (chunk(b, d, s), 2 + d)),
            pl.BlockSpec((c, hw), lambda b, d, s: (chunk(b, d, s), 4)),
            pl.BlockSpec((1, 1, hw), lambda b, d, s: (d * depth + layer, 0, 0)),
            pl.BlockSpec((1, n_rows, 2 * c), lambda b, d, s: (d, 0, 0)),
        ],
        out_specs=pl.BlockSpec((1, c, hw), lambda b, d, s: (d, chunk(b, d, s), 0)),
        out_shape=jax.ShapeDtypeStruct((2, ntot, hw), BF16),
        scratch_shapes=[pltpu.VMEM((n_heads, HEAD_DIM, HEAD_DIM), F32), pltpu.VMEM((n_rows, hw), F32),
                        pltpu.VMEM((n_heads, c, c), BF16)],
        compiler_params=_cparams(3),
    )(z, z, z, lb, mats)


def _pool_constants(tile, n_ctx):
    assert n_ctx == tile and tile % GRID_W == 0
    band = np.zeros((2, len(POOL_WINDOWS), tile, tile), np.float32)
    inv = np.zeros((2, len(POOL_WINDOWS), tile, 1), np.float32)
    for kind, row_len in enumerate((n_ctx, GRID_W)):
        t = np.arange(tile)
        tau = t % row_len
        base = t - tau
        for g, w in enumerate(POOL_WINDOWS):
            lo = np.maximum(tau - w // 2, 0)
            hi = np.minimum(tau + w // 2 - 1, row_len - 1)
            s = np.arange(tile)[None, :]
            band[kind, g] = (s >= (base + lo)[:, None]) & (s <= (base + hi)[:, None])
            inv[kind, g, :, 0] = 1.0 / (hi - lo + 1)
    return band, inv


def _route(sel, sc):
    n_g, per = N_EXPERT_GROUPS, EXPERTS_PER_GROUP
    row = lambda a, e: a[e:e + 1, :]
    best = None
    for g in range(n_g):
        rows = [row(sel, g * per + i) for i in range(per)]
        gs = None
        for i in range(per):
            for j in range(i + 1, per):
                p = rows[i] + rows[j]
                gs = p if gs is None else jnp.maximum(gs, p)
        if best is None:
            best, g_idx = gs, jnp.zeros(gs.shape, jnp.int32)
        else:
            upd = gs > best
            g_idx = jnp.where(upd, g, g_idx)
            best = jnp.where(upd, gs, best)
    cand, raw = [], []
    for i in range(per):
        ci, ri = row(sel, i), row(sc, i)
        for g in range(1, n_g):
            ci = jnp.where(g_idx == g, row(sel, g * per + i), ci)
            ri = jnp.where(g_idx == g, row(sc, g * per + i), ri)
        cand.append(ci)
        raw.append(ri)

    def argmax4(vals):
        m, loc = vals[0], jnp.zeros(vals[0].shape, jnp.int32)
        for i in range(1, per):
            upd = vals[i] > m
            loc = jnp.where(upd, i, loc)
            m = jnp.where(upd, vals[i], m)
        return loc

    loc0 = argmax4(cand)
    loc1 = argmax4([jnp.where(loc0 == i, -jnp.inf, cand[i]) for i in range(per)])

    def pick(loc):
        out = raw[0]
        for i in range(1, per):
            out = jnp.where(loc == i, raw[i], out)
        return out

    g0, g1 = pick(loc0), pick(loc1)
    tot = g0 + g1
    return g_idx * per + loc0, g_idx * per + loc1, g0 / tot, g1 / tot


def _merge_kernel(d, hw, tiles_per_sample, ctx_row,
                  x_ref, p_ref, og_ref, gp_ref, gh_ref, o_ref, mod_ref, n2_ref, ps_ref, hn_ref,
                  band_ref, inv_ref, wpg_ref, wbp_ref, wbh_ref, wout_ref, wr_ref, rb_ref, tri_ref,
                  xo_ref, h2_ref, eidx_ref, gate_ref, cnt_ref, pool_scr, hg_scr, carry_ref):
    i = pl.program_id(0)
    n_heads = hw // HEAD_DIM
    n_groups = len(POOL_WINDOWS)
    pg = p_ref.shape[1] // n_groups
    mrow = _mod_row(i, tiles_per_sample, ctx_row)
    mod = lambda k: mod_ref[pl.ds(mrow, 1), k * d:(k + 1) * d]

    for h in range(n_heads):
        cs = slice(h * HEAD_DIM, (h + 1) * HEAD_DIM)
        o = o_ref[0, :, cs].astype(F32) + o_ref[1, :, cs].astype(F32)
        o = o * lax.rsqrt(jnp.mean(o * o, axis=-1, keepdims=True) + EPS)
        og = og_ref[:, cs].astype(F32)
        hg_scr[:, cs] = (o * hn_ref[:, cs] * (og * _sigmoid(og))).astype(BF16)

    for g in range(n_groups):
        cs = slice(g * pg, (g + 1) * pg)
        u = p_ref[:, cs]
        wsum = _dot(band_ref[0, g], u)
        dlt = wsum * inv_ref[0, g] - u.astype(F32)
        pool_scr[:, cs] = (_dot(dlt.astype(BF16), wpg_ref[g]) * ps_ref[:, cs]).astype(BF16)

    gp = gp_ref[...].astype(F32)
    gh = gh_ref[...].astype(F32)
    m = _sigmoid(gp) * _dot(pool_scr[...], wbp_ref[...]) + _sigmoid(gh) * _dot(hg_scr[...], wbh_ref[...])
    y = _dot(m.astype(BF16), wout_ref[...])
    xn = x_ref[...] + mod(2) * y
    xo_ref[...] = xn
    n2 = xn * lax.rsqrt(jnp.mean(xn * xn, axis=-1, keepdims=True) + EPS) * n2_ref[...]
    h2 = n2 * (1.0 + mod(4)) + mod(3)
    h2_ref[...] = h2

    h_hi = h2.astype(BF16)
    h_lo = (h2 - h_hi.astype(F32)).astype(BF16)
    wr = wr_ref[...]
    w_hi = wr.astype(BF16)
    w_lo = (wr - w_hi.astype(F32)).astype(BF16)
    logits = _dot_nt(w_hi, h_hi) + (_dot_nt(w_hi, h_lo) + _dot_nt(w_lo, h_hi))
    sc = _sigmoid(logits)
    e0, e1, g0, g1 = _route(sc + rb_ref[...], sc)

    @pl.when(i == 0)
    def _():
        carry_ref[...] = jnp.zeros_like(carry_ref)

    er = lax.broadcasted_iota(jnp.int32, sc.shape, 0)
    oh0 = (er == e0).astype(F32)
    oh1 = (er == e1).astype(F32)
    both = oh0 + oh1
    base = carry_ref[:, 0:1] + _dot(both.astype(BF16), tri_ref[...])
    r0 = jnp.sum(oh0 * base, axis=0, keepdims=True)
    r1 = jnp.sum(oh1 * base, axis=0, keepdims=True)
    carry_ref[...] = carry_ref[...] + jnp.sum(both, axis=1, keepdims=True)
    cnt_ref[...] = carry_ref[...].astype(jnp.int32)

    eidx_ref[...] = jnp.zeros_like(eidx_ref)
    gate_ref[...] = jnp.zeros_like(gate_ref)
    eidx_ref[0, 0:1, :] = e0
    eidx_ref[0, 1:2, :] = e1
    eidx_ref[0, 2:3, :] = r0.astype(jnp.int32)
    eidx_ref[0, 3:4, :] = r1.astype(jnp.int32)
    gate_ref[0, 0:1, :] = g0
    gate_ref[0, 1:2, :] = g1


def _const_spec(shape):
    nd = len(shape)
    return pl.BlockSpec(shape, lambda i: (0,) * nd, pipeline_mode=pl.Buffered(1))


def _merge(x, z, o2, mods_l, norm2_w, pool_scale, hg_norm_w, w_pg, w_bp, w_bh, w_out, w_router_t,
           router_bias, n_ctx, t_sample, ctx_row):
    ntot, d = x.shape
    hw = o2.shape[2]
    d_pool = w_bp.shape[0]
    assert d_pool == hw and d == 2 * hw
    tm = ROW_TILE
    tps = t_sample // tm
    n_e = w_router_t.shape[0]
    band, inv = _pool_constants(tm, n_ctx)
    band = jnp.asarray(band, BF16)
    inv = jnp.asarray(inv, F32)
    kind = lambda i: jnp.where(i % tps == 0, 0, 1)
    n_tiles = ntot // tm
    before = jnp.asarray(np.triu(np.ones((tm, tm), np.float32), 1), BF16)
    return pl.pallas_call(
        functools.partial(_merge_kernel, d, hw, tps, ctx_row),
        name="merge",
        grid=(n_tiles,),
        in_specs=[
            pl.BlockSpec((tm, d), lambda i: (i, 0)),
            pl.BlockSpec((tm, hw), lambda i: (i, 0)),
            pl.BlockSpec((tm, hw), lambda i: (i, 5)),
            pl.BlockSpec((tm, d), lambda i: (i, 3)),
            pl.BlockSpec((tm, d), lambda i: (i, 4)),
            pl.BlockSpec((2, tm, hw), lambda i: (0, i, 0)),
            _const_spec((8, 6 * d)),
            _const_spec((1, d)),
            _const_spec((1, hw)),
            _const_spec((1, hw)),
            pl.BlockSpec((1,) + band.shape[1:], lambda i: (kind(i), 0, 0, 0)),
            pl.BlockSpec((1,) + inv.shape[1:], lambda i: (kind(i), 0, 0, 0)),
            _const_spec(w_pg.shape),
            _const_spec(w_bp.shape),
            _const_spec(w_bh.shape),
            _const_spec(w_out.shape),
            _const_spec(w_router_t.shape),
            _const_spec((n_e, 1)),
            _const_spec((tm, tm)),
        ],
        out_specs=[
            pl.BlockSpec((tm, d), lambda i: (i, 0)),
            pl.BlockSpec((tm, d), lambda i: (i, 0)),
            pl.BlockSpec((1, 8, tm), lambda i: (i, 0, 0)),
            pl.BlockSpec((1, 8, tm), lambda i: (i, 0, 0)),
            pl.BlockSpec((n_e, 128), lambda i: (0, 0)),
        ],
        out_shape=[
            jax.ShapeDtypeStruct((ntot, d), F32),
            jax.ShapeDtypeStruct((ntot, d), F32),
            jax.ShapeDtypeStruct((n_tiles, 8, tm), jnp.int32),
            jax.ShapeDtypeStruct((n_tiles, 8, tm), F32),
            jax.ShapeDtypeStruct((n_e, 128), jnp.int32),
        ],
        scratch_shapes=[pltpu.VMEM((tm, hw), BF16), pltpu.VMEM((tm, hw), BF16), pltpu.VMEM((n_e, 128), F32)],
        compiler_params=_cparams(1),
    )(x, z, z, z, z, o2, mods_l, norm2_w.reshape(1, d), pool_scale.reshape(1, hw),
      hg_norm_w.reshape(1, hw), band, inv, w_pg, w_bp, w_bh, w_out, w_router_t,
      router_bias.astype(F32).reshape(n_e, 1), before)


def _row_gather_start(src_hbm, idx_ref, dst, sem, n_rows):
    def body(it, carry):
        for k in range(2):
            r = 2 * it + k
            tok = idx_ref[0, 0, r]
            pltpu.make_async_copy(src_hbm.at[pl.ds(tok, 1)], dst.at[pl.ds(r, 1)], sem).start(priority=k)
        return carry

    lax.fori_loop(0, n_rows // 2, body, 0, unroll=4)


def _row_gather_wait(src_hbm, dst, sem, n_rows):
    pltpu.make_async_copy(src_hbm.at[pl.ds(0, n_rows)], dst, sem).wait()


def _dispatch_kernel(n_e, n_blk, tails_ref, pos_ref, h_ref, xs_hbm, zblk, sems):
    tm = h_ref.shape[0]
    rows = zblk.shape[0]
    sem = sems.at[0]
    zsem = sems.at[1]

    def for_each_pad(act):
        for e in range(n_e):
            def body(r, carry, e=e):
                act(pltpu.make_async_copy(zblk.at[pl.ds(0, 1)], xs_hbm.at[pl.ds(tails_ref[e] + r, 1)], zsem))
                return carry

            lax.fori_loop(0, tails_ref[n_e + e], body, 0)

        def blk_body(b, carry):
            act(pltpu.make_async_copy(zblk, xs_hbm.at[pl.ds(pl.multiple_of(b * rows, rows), rows)], zsem))
            return carry

        lax.fori_loop(tails_ref[2 * n_e], n_blk, blk_body, 0)

    @pl.when(pl.program_id(0) == 0)
    def _():
        zblk[...] = jnp.zeros_like(zblk)
        for_each_pad(lambda cp: cp.start())

    def body(r, carry):
        for k in range(2):
            dst = pos_ref[0, 0, k * tm + r]
            pltpu.make_async_copy(h_ref.at[pl.ds(r, 1)], xs_hbm.at[pl.ds(dst, 1)], sem).start(priority=k)
        return carry

    lax.fori_loop(0, tm, body, 0, unroll=8)
    pltpu.make_async_copy(xs_hbm.at[pl.ds(0, 2 * tm)], xs_hbm.at[pl.ds(0, 2 * tm)], sem).wait()

    @pl.when(pl.program_id(0) == 0)
    def _():
        for_each_pad(lambda cp: cp.wait())


def _dispatch(h2, pos, tails, n_blk):
    ntot, d = h2.shape
    tm = ROW_TILE
    n_tiles = ntot // tm
    n_e = (tails.shape[0] - 1) // 2
    grid_spec = pltpu.PrefetchScalarGridSpec(
        num_scalar_prefetch=1,
        grid=(n_tiles,),
        in_specs=[
            pl.BlockSpec((1, 1, 2 * tm), lambda i, t: (i, 0, 0), memory_space=pltpu.SMEM),
            pl.BlockSpec((tm, d), lambda i, t: (i, 0)),
        ],
        out_specs=pl.BlockSpec(memory_space=pl.ANY),
        scratch_shapes=[pltpu.VMEM((MOE_ROWS, d), F32), pltpu.SemaphoreType.DMA((2,))],
    )
    return pl.pallas_call(
        functools.partial(_dispatch_kernel, n_e, n_blk),
        name="dispatch",
        grid_spec=grid_spec,
        out_shape=jax.ShapeDtypeStruct((n_blk * MOE_ROWS, d), F32),
        compiler_params=_cparams(1),
    )(tails, pos, h2)


def _expert_kernel(blk_e_ref, nused_ref, x_ref, w1_ref, w3_ref, w2_ref, y_ref, w1c, w3c, w2c):
    i = pl.program_id(0)
    n_used = nused_ref[0]

    @pl.when(i < n_used)
    def _():
        changed = jnp.logical_or(i == 0, blk_e_ref[i] != blk_e_ref[jnp.maximum(i - 1, 0)])

        @pl.when(changed)
        def _():
            w1c[...] = w1_ref[0, 0].astype(BF16)
            w3c[...] = w3_ref[0, 0].astype(BF16)
            w2c[...] = w2_ref[0, 0].astype(BF16)

        xb = x_ref[...].astype(BF16)
        h1 = _dot(xb, w1c[...])
        h3 = _dot(xb, w3c[...])
        act = (h1 * _sigmoid(h1) * h3).astype(BF16)
        y_ref[...] = _dot(act, w2c[...])

    @pl.when(i >= n_used)
    def _():
        y_ref[...] = jnp.zeros_like(y_ref)


def _experts(xs, blk_e, n_used, w1, w3, w2, layer):
    n_rows, d = xs.shape
    _, n_e, _, f = w1.shape
    n_blk = blk_e.shape[0]
    rows = MOE_ROWS
    used = lambda i, nu: jnp.minimum(i, nu[0] - 1)
    grid_spec = pltpu.PrefetchScalarGridSpec(
        num_scalar_prefetch=2,
        grid=(n_blk,),
        in_specs=[
            pl.BlockSpec((rows, d), lambda i, be, nu: (used(i, nu), 0)),
            pl.BlockSpec((1, 1, d, f), lambda i, be, nu: (layer, be[i], 0, 0)),
            pl.BlockSpec((1, 1, d, f), lambda i, be, nu: (layer, be[i], 0, 0)),
            pl.BlockSpec((1, 1, f, d), lambda i, be, nu: (layer, be[i], 0, 0)),
        ],
        out_specs=pl.BlockSpec((rows, d), lambda i, be, nu: (i, 0)),
        scratch_shapes=[
            pltpu.VMEM((d, f), BF16),
            pltpu.VMEM((d, f), BF16),
            pltpu.VMEM((f, d), BF16),
        ],
    )
    return pl.pallas_call(
        _expert_kernel,
        name="experts",
        grid_spec=grid_spec,
        out_shape=jax.ShapeDtypeStruct((n_rows, d), F32),
        compiler_params=_cparams(1),
    )(blk_e, n_used, xs, w1, w3, w2)


def _combine_kernel(d, tiles_per_sample, ctx_row, emit_h, pos0_ref, posn_ref, y_hbm, x_ref, g_ref, mod_ref,
                    *rest):
    if emit_h:
        modn_ref, nw_ref, xo_ref, h_ref, rbuf, sem = rest
    else:
        xo_ref, rbuf, sem = rest
    i = pl.program_id(0)
    n = pl.num_programs(0)
    tm = x_ref.shape[0]
    slot = i % 2

    @pl.when(i == 0)
    def _():
        _row_gather_start(y_hbm, pos0_ref, rbuf.at[0], sem.at[0], 2 * tm)

    @pl.when(i + 1 < n)
    def _():
        _row_gather_start(y_hbm, posn_ref, rbuf.at[1 - slot], sem.at[1 - slot], 2 * tm)

    _row_gather_wait(y_hbm, rbuf.at[slot], sem.at[slot], 2 * tm)
    mrow = _mod_row(i, tiles_per_sample, ctx_row)
    m5 = mod_ref[pl.ds(mrow, 1), 5 * d:6 * d]
    g = g_ref[...]
    f = g[:, 0:1] * rbuf[slot, 0:tm, :] + g[:, 1:2] * rbuf[slot, tm:2 * tm, :]
    xn = x_ref[...] + m5 * f
    xo_ref[...] = xn
    if emit_h:
        shift = modn_ref[pl.ds(mrow, 1), 0:d]
        scale = modn_ref[pl.ds(mrow, 1), d:2 * d]
        h_ref[...] = _norm_mod(xn, nw_ref[...], shift, scale).astype(BF16)


def _combine(x, yb, pos, gates, mods_l, t_sample, ctx_row, mods_next=None, norm_next=None):
    ntot, d = x.shape
    tm = ROW_TILE
    n_tiles = ntot // tm
    emit_h = mods_next is not None
    in_specs = [
        pl.BlockSpec((1, 1, 2 * tm), lambda i: (0, 0, 0), memory_space=pltpu.SMEM),
        pl.BlockSpec((1, 1, 2 * tm), lambda i: (jnp.minimum(i + 1, n_tiles - 1), 0, 0),
                     memory_space=pltpu.SMEM),
        pl.BlockSpec(memory_space=pl.ANY),
        pl.BlockSpec((tm, d), lambda i: (i, 0)),
        pl.BlockSpec((tm, 2), lambda i: (i, 0)),
        _const_spec((8, 6 * d)),
    ]
    args = [pos, pos, yb, x, gates, mods_l]
    out_specs = [pl.BlockSpec((tm, d), lambda i: (i, 0))]
    out_shape = [jax.ShapeDtypeStruct((ntot, d), F32)]
    if emit_h:
        in_specs += [pl.BlockSpec((8, 2 * d), lambda i: (0, 0)), pl.BlockSpec((1, d), lambda i: (0, 0))]
        args += [mods_next, norm_next.reshape(1, d)]
        out_specs.append(pl.BlockSpec((tm, d), lambda i: (i, 0)))
        out_shape.append(jax.ShapeDtypeStruct((ntot, d), BF16))
    return pl.pallas_call(
        functools.partial(_combine_kernel, d, t_sample // tm, ctx_row, emit_h),
        name="combine",
        grid=(n_tiles,),
        in_specs=in_specs,
        out_specs=out_specs,
        out_shape=out_shape,
        scratch_shapes=[pltpu.VMEM((2, 2 * tm, d), F32), pltpu.SemaphoreType.DMA((2,))],
        compiler_params=_cparams(1),
    )(*args)


def _final_kernel(x_ref, w_ref, o_ref):
    x = x_ref[...]
    o_ref[...] = x * lax.rsqrt(jnp.mean(x * x, axis=-1, keepdims=True) + EPS) * w_ref[...]


def _final_norm(x, w, n_batch, n_ctx, n_lat):
    ntot, d = x.shape
    tm = ROW_TILE
    tps = (n_ctx + n_lat) // tm
    cps = n_ctx // tm
    lps = n_lat // tm
    return pl.pallas_call(
        _final_kernel,
        name="final_norm",
        grid=(n_batch * lps,),
        in_specs=[
            pl.BlockSpec((tm, d), lambda i: ((i // lps) * tps + cps + i % lps, 0)),
            pl.BlockSpec((1, d), lambda i: (0, 0)),
        ],
        out_specs=pl.BlockSpec((tm, d), lambda i: (i, 0)),
        out_shape=jax.ShapeDtypeStruct((n_batch * n_lat, d), F32),
        compiler_params=_cparams(1),
    )(x, w.reshape(1, d))


def _dispatch_plan(eidx, counts, n_e):
    rows = MOE_ROWS
    n_tiles, _, tm = eidx.shape
    n_assign = n_tiles * tm * 2
    pcounts = (counts + rows - 1) // rows * rows
    pends = jnp.cumsum(pcounts)
    pstart = pends - pcounts
    experts = jnp.arange(n_e, dtype=jnp.int32)
    first = jnp.sum(jnp.where(eidx[:, 0:2, :, None] == experts, pstart, 0), axis=-1)
    dest = first + eidx[:, 2:4, :]
    n_blk = -(-n_assign // rows) + n_e
    blk_row0 = jnp.arange(n_blk, dtype=jnp.int32) * rows
    blk_e = jnp.minimum(jnp.sum(blk_row0[:, None] >= pends[None, :], axis=1), n_e - 1).astype(jnp.int32)
    n_used = (pends[-1] // rows).astype(jnp.int32).reshape(1)
    pos = dest.astype(jnp.int32).reshape(n_tiles, 1, 2 * tm)
    tails = jnp.concatenate([pstart + counts, pcounts - counts, n_used]).astype(jnp.int32)
    return blk_e, n_used, pos, n_blk, tails


def kernel(x, c, ctx, c_ctx, w_ada, b_ada, norm1_w, norm2_w, w_in, w_pool_group, pool_scale, hg_norm_w,
           lb_logits, w_branch_pool, w_branch_hgrn, w_out, w_router, router_bias, w_e1, w_e3, w_e2,
           final_norm_w):
    n_batch, n_lat, d = x.shape
    n_ctx = ctx.shape[1]
    depth = w_ada.shape[0]
    hw = hg_norm_w.shape[1]
    n_e = w_router.shape[1]
    t_sample = n_ctx + n_lat
    ntot = n_batch * t_sample
    ctx_row = n_batch
    assert n_batch < 8 and n_ctx == ROW_TILE and n_lat % ROW_TILE == 0

    xs = jnp.concatenate([ctx, x], axis=1).reshape(ntot, d)
    cc = jnp.zeros((8, d), F32).at[:n_batch].set(c).at[ctx_row].set(c_ctx)
    mods = _ada_mods(cc, w_ada, b_ada)
    lb = _lower_bounds(lb_logits)
    w_router_t = jnp.transpose(w_router.astype(F32))

    h = _norm_mod_rows(xs, mods[0], norm1_w[0], t_sample, ctx_row)
    for l in range(depth):
        z = _in_proj(h, w_in, l)
        o2 = _scan(z, lb, l, depth, n_batch, n_ctx, n_lat, hw)
        xs, h2, eidx, gate, counts = _merge(
            xs, z, o2, mods[l], norm2_w[l], pool_scale[l], hg_norm_w[l], w_pool_group[l].astype(BF16),
            w_branch_pool[l].astype(BF16), w_branch_hgrn[l].astype(BF16), w_out[l].astype(BF16),
            w_router_t, router_bias, n_ctx, t_sample, ctx_row)
        blk_e, n_used, pos, n_blk, tails = _dispatch_plan(eidx, counts[:, 0], n_e)
        yb = _experts(_dispatch(h2, pos, tails, n_blk), blk_e, n_used, w_e1, w_e3, w_e2, l)
        g = jnp.transpose(gate[:, 0:2, :], (0, 2, 1)).reshape(ntot, 2)
        if l + 1 < depth:
            xs, h = _combine(xs, yb, pos, g, mods[l], t_sample, ctx_row, mods[l + 1], norm1_w[l + 1])
        else:
            xs, = _combine(xs, yb, pos, g, mods[l], t_sample, ctx_row)

    out = _final_norm(xs, final_norm_w, n_batch, n_ctx, n_lat)
    return out.reshape(n_batch, n_lat, d)
```

```python
import functools
import math

import numpy as np
import jax
import jax.numpy as jnp
from jax import lax
from jax.experimental import pallas as pl
from jax.experimental.pallas import tpu as pltpu

F32 = jnp.float32
BF16 = jnp.bfloat16

EPS = 1e-6
GRID_W = 64
POOL_WINDOWS = (2, 4, 8, 16)
HEAD_DIM = 128
N_EXPERT_GROUPS = 4
EXPERTS_PER_GROUP = 4
SCAN_CHUNK = 64
ROW_TILE = 256
MOE_ROWS = 256
V7X_VMEM_BYTES = 64 * 1024 * 1024
VMEM_LIMIT = V7X_VMEM_BYTES - 8 * 1024 * 1024


def _cparams(n_axes, vmem=VMEM_LIMIT):
    return pltpu.CompilerParams(dimension_semantics=("arbitrary",) * n_axes, vmem_limit_bytes=vmem)


def _sigmoid(x):
    return 1.0 / (1.0 + jnp.exp(-x))


def _dot(a, b):
    return jnp.dot(a, b, preferred_element_type=F32)


def _dot_nt(a, b):
    return lax.dot_general(a, b, (((1,), (1,)), ((), ())), preferred_element_type=F32)


def _pack_halves(x):
    half = x.shape[1] // 2
    lo = lax.bitcast_convert_type(x[:, :half].astype(BF16).astype(F32), jnp.uint32)
    hi = lax.bitcast_convert_type(x[:, half:].astype(BF16).astype(F32), jnp.uint32)
    return (lo >> 16) | (hi & jnp.uint32(0xFFFF0000))


def _unpack_halves(w):
    lo = lax.bitcast_convert_type(w << 16, F32)
    hi = lax.bitcast_convert_type(w & jnp.uint32(0xFFFF0000), F32)
    return lo, hi


def _lb_kernel(depth, logit_ref, lb_ref):
    for d in range(2):
        rows = [logit_ref[d * depth + i:d * depth + i + 1, :] for i in range(depth)]
        m = rows[0]
        for r in rows[1:]:
            m = jnp.maximum(m, r)
        es = [jnp.exp(r - m) for r in rows]
        tot = es[0]
        for e in es[1:]:
            tot = tot + e
        ws = [e / tot for e in es]
        acc = jnp.zeros_like(ws[0])
        for i in range(depth):
            acc = acc + ws[i]
            lb_ref[d * depth + i:d * depth + i + 1, :] = acc - ws[0]


def _lower_bounds(lb_logits):
    _, depth, hw = lb_logits.shape
    flat = lb_logits.astype(F32).reshape(2 * depth, hw)
    out = pl.pallas_call(
        functools.partial(_lb_kernel, depth),
        name="lower_bounds",
        out_shape=jax.ShapeDtypeStruct((2 * depth, hw), F32),
    )(flat)
    return out.reshape(2 * depth, 1, hw)


def _ada_kernel(c_ref, w_ref, b_ref, o_ref):
    c = c_ref[...]
    s = (c * _sigmoid(c)).astype(BF16)
    o_ref[0] = _dot(s, w_ref[0].astype(BF16)) + b_ref[0]


def _ada_mods(cc, w_ada, b_ada):
    depth, d, n6 = w_ada.shape
    tn = 1024
    return pl.pallas_call(
        _ada_kernel,
        name="ada_mods",
        grid=(depth, n6 // tn),
        in_specs=[
            pl.BlockSpec((8, d), lambda l, j: (0, 0)),
            pl.BlockSpec((1, d, tn), lambda l, j: (l, 0, j)),
            pl.BlockSpec((1, 1, tn), lambda l, j: (l, 0, j)),
        ],
        out_specs=pl.BlockSpec((1, 8, tn), lambda l, j: (l, 0, j)),
        out_shape=jax.ShapeDtypeStruct((depth, 8, n6), F32),
        compiler_params=_cparams(2),
    )(cc, w_ada, b_ada.reshape(depth, 1, n6))


INPROJ_MAX_ROWS = 2176
INPROJ_COLS = 1024


def _mod_row(i, tiles_per_sample, ctx_row):
    return jnp.where(i % tiles_per_sample == 0, ctx_row, i // tiles_per_sample)


def _norm_mod(x, nw, shift, scale):
    n = x * lax.rsqrt(jnp.mean(x * x, axis=-1, keepdims=True) + EPS) * nw
    return n * (1.0 + scale) + shift


def _normmod_kernel(d, tiles_per_sample, ctx_row, x_ref, c_ref, mod_ref, nw_ref, xs_ref, h_ref):
    i = pl.program_id(0)
    mrow = _mod_row(i, tiles_per_sample, ctx_row)
    shift = mod_ref[pl.ds(mrow, 1), 0:d]
    scale = mod_ref[pl.ds(mrow, 1), d:2 * d]

    def emit(src_ref):
        x = src_ref[...]
        xs_ref[...] = x
        h_ref[...] = _norm_mod(x, nw_ref[...], shift, scale).astype(BF16)

    is_ctx = i % tiles_per_sample == 0
    pl.when(is_ctx)(lambda: emit(c_ref))
    pl.when(jnp.logical_not(is_ctx))(lambda: emit(x_ref))


def _norm_mod_rows(x, ctx, mods_l, norm_w):
    n_batch, n_lat, d = x.shape
    n_ctx = ctx.shape[1]
    tm = ROW_TILE
    assert n_ctx == tm
    lps = n_lat // tm
    tps = lps + 1
    ntot = n_batch * (n_ctx + n_lat)
    return pl.pallas_call(
        functools.partial(_normmod_kernel, d, tps, n_batch),
        name="norm_mod",
        grid=(n_batch * tps,),
        in_specs=[
            pl.BlockSpec((tm, d), lambda i: ((i // tps) * lps + jnp.maximum(i % tps - 1, 0), 0)),
            pl.BlockSpec((tm, d), lambda i: (i // tps, 0)),
            pl.BlockSpec((8, 2 * d), lambda i: (0, 0)),
            pl.BlockSpec((1, d), lambda i: (0, 0)),
        ],
        out_specs=[pl.BlockSpec((tm, d), lambda i: (i, 0))] * 2,
        out_shape=[jax.ShapeDtypeStruct((ntot, d), F32), jax.ShapeDtypeStruct((ntot, d), BF16)],
        compiler_params=_cparams(1),
    )(x.reshape(n_batch * n_lat, d), ctx.reshape(n_batch * n_ctx, d), mods_l, norm_w.reshape(1, d))


def _inproj_kernel(h_ref, w_ref, o_ref):
    o_ref[...] = _dot(h_ref[...], w_ref[0].astype(BF16)).astype(BF16)


def _in_proj(h, w_in, layer):
    ntot, d = h.shape
    n_cols = w_in.shape[2]
    tn = INPROJ_COLS
    tm = max(k for k in range(16, INPROJ_MAX_ROWS + 1, 16) if ntot % k == 0)
    return pl.pallas_call(
        _inproj_kernel,
        name="in_proj",
        grid=(ntot // tm, n_cols // tn),
        in_specs=[
            pl.BlockSpec((tm, d), lambda r, j: (r, 0)),
            pl.BlockSpec((1, d, tn), lambda r, j: (layer, 0, j)),
        ],
        out_specs=pl.BlockSpec((tm, tn), lambda r, j: (r, j)),
        out_shape=jax.ShapeDtypeStruct((ntot, n_cols), BF16),
        compiler_params=_cparams(2),
    )(h, w_in)


COMPACT_MIN_HALF = 8
MXU_LEVEL_MIN_HALF = 2


def _scan_levels(c):
    out, m = [], c // 2
    while m >= MXU_LEVEL_MIN_HALF:
        out.append(m)
        m //= 2
    return out


def _scan_row_layout(c):
    off = {"b": 0, "kend": c, "tot": 2 * c}
    pos = 2 * c + 8
    for m in _scan_levels(c):
        n = c // 2 if m >= COMPACT_MIN_HALF else c
        off[("q", m)] = pos
        off[("k", m)] = pos + n
        pos += 2 * n
    return off, pos


def _scan_constants(c):
    t = np.arange(c)[:, None]
    r = np.arange(c)[None, :]
    out = []
    for flip in (False, True):
        fl = (lambda a: a[::-1, ::-1]) if flip else (lambda a: a)
        sets = [fl(r <= t), fl(r > t), np.ones((8, c), bool)]
        for m in _scan_levels(c):
            mid = (t // (2 * m)) * (2 * m) + m
            qa = fl((t >= mid) & (r >= mid) & (r <= t))
            ka = fl((t < mid) & (r > t) & (r <= mid - 1))
            if m >= COMPACT_MIN_HALF:
                rows = np.arange(c)
                q_valid = (rows % (2 * m) >= m) != flip
                sets += [qa[q_valid], ka[~q_valid]]
            else:
                sets += [qa, ka]
        mat = np.concatenate(sets, axis=0).astype(np.float32)
        out.append(np.concatenate([mat, mat], axis=1))
    return np.stack(out)


def _tile_pairs(e, m, c):
    parts = []
    for p in range(c // (2 * m)):
        blk = e[p * m:(p + 1) * m]
        parts += [blk, blk]
    return jnp.concatenate(parts, axis=0)


def _scan_gates(c, q_ref, f_ref, lb_ref, mat, e_ref):
    zq = q_ref[...].astype(F32)
    zf = f_ref[...].astype(F32)
    lbv = lb_ref[0]
    f = lbv + (1.0 - lbv) * _sigmoid(zf)
    kk = 1.0 - f
    qq = zq * _sigmoid(zq)
    lf2 = jnp.log(f) * math.log2(math.e)
    lf_hi = lf2.astype(BF16)
    lf_lo = (lf2 - lf_hi.astype(F32)).astype(BF16)
    e_ref[...] = jnp.exp2(_dot(mat, jnp.concatenate([lf_hi, lf_lo], axis=0)))
    row = lax.broadcasted_iota(jnp.int32, (c, 1), 0)
    k_sw = jnp.where((row & 1) == 1, pltpu.roll(kk, 1, axis=0), pltpu.roll(kk, c - 1, axis=0))
    return qq, kk, qq * f, k_sw


def _scan_intra(n_heads, c, backward, qq, kk, qf, k_sw, e_ref, a_ref):
    levels = _scan_levels(c)
    off, _ = _scan_row_layout(c)
    ti = lax.broadcasted_iota(jnp.int32, (c, c), 0)
    si = lax.broadcasted_iota(jnp.int32, (c, c), 1)
    same = {m: (ti & -(2 * m)) == (si & -(2 * m)) for m in levels[1:] + [1]}
    eye = ti == si
    causal = (ti <= si) if backward else (ti >= si)
    for h in range(n_heads):
        cs = slice(h * HEAD_DIM, (h + 1) * HEAD_DIM)
        qh = qq[:, cs]
        kh = kk[:, cs]
        a = None
        for m in levels:
            n = c // 2 if m >= COMPACT_MIN_HALF else c
            eq = e_ref[off[("q", m)]:off[("q", m)] + n, cs]
            ek = e_ref[off[("k", m)]:off[("k", m)] + n, cs]
            if m >= COMPACT_MIN_HALF:
                eq, ek = _tile_pairs(eq, m, c), _tile_pairs(ek, m, c)
            p = _dot_nt((qh * eq).astype(BF16), (kh * ek).astype(BF16))
            a = p if a is None else jnp.where(same[m], p, a)
        pair = jnp.sum(qf[:, cs] * k_sw[:, cs], axis=-1, keepdims=True)
        diag = jnp.sum(qh * kh, axis=-1, keepdims=True)
        a = jnp.where(same[1], pair, a)
        a = jnp.where(eye, diag, a)
        a_ref[h] = jnp.where(causal, a, 0.0).astype(BF16)


def _scan_state(n_heads, c, qq, kk, i_ref, e_ref, a_ref, s_ref, o_ref):
    off, _ = _scan_row_layout(c)
    for h in range(n_heads):
        cs = slice(h * HEAD_DIM, (h + 1) * HEAD_DIM)
        qh = qq[:, cs]
        kh = kk[:, cs]
        vh = i_ref[:, cs]
        st = s_ref[h]
        qb = (qh * e_ref[off["b"]:off["b"] + c, cs]).astype(BF16)
        o = _dot_nt(qb, st.astype(BF16)) + _dot(a_ref[h], vh)
        o_ref[:, cs] = o.astype(BF16)
        vt = vh.astype(F32).T.astype(BF16)
        kt = (kh * e_ref[off["kend"]:off["kend"] + c, cs]).astype(BF16)
        s_ref[h] = st * e_ref[off["tot"]:off["tot"] + 1, cs] + _dot(vt, kt)


def _scan_kernel(n_heads, c, qf_ref, qb_ref, ff_ref, fb_ref, if_ref, ib_ref, lbf_ref, lbb_ref, m_ref,
                 of_ref, ob_ref, s_ref, e_ref, a_ref):
    @pl.when(pl.program_id(1) == 0)
    def _():
        s_ref[...] = jnp.zeros_like(s_ref)

    dirs = ((qf_ref, ff_ref, if_ref, lbf_ref, of_ref), (qb_ref, fb_ref, ib_ref, lbb_ref, ob_ref))
    gates = [_scan_gates(c, q, f, lb, m_ref[d], e_ref.at[d]) for d, (q, f, _, lb, _) in enumerate(dirs)]
    for d in range(2):
        _scan_intra(n_heads, c, d == 1, *gates[d], e_ref.at[d], a_ref.at[d])
    for d, (_, _, i_ref, _, o_ref) in enumerate(dirs):
        _scan_state(n_heads, c, gates[d][0], gates[d][1], i_ref, e_ref.at[d], a_ref.at[d], s_ref.at[d], o_ref)


def _scan(z, lb, layer, depth, n_batch, n_ctx, n_lat, hw):
    c = SCAN_CHUNK
    n_heads = hw // HEAD_DIM
    nc, nl = n_ctx // c, n_lat // c
    steps = nc + nl
    mats = jnp.asarray(_scan_constants(c), BF16)
    n_rows = mats.shape[1]
    assert n_rows == _scan_row_layout(c)[1]
    ntot = z.shape[0]

    def fwd(b, s):
        return b * steps + s

    def bwd(b, s):
        return b * steps + jnp.where(s < nc, nc - 1 - s, nc + (steps - 1 - s))

    def col(chunk, j):
        return pl.BlockSpec((c, hw), lambda b, s: (chunk(b, s), j))

    lb_spec = lambda d: pl.BlockSpec((1, 1, hw), lambda b, s: (d * depth + layer, 0, 0))
    return pl.pallas_call(
        functools.partial(_scan_kernel, n_heads, c),
        name="hgrn2_scan",
        grid=(n_batch, steps),
        in_specs=[
            col(fwd, 1), col(bwd, 1),
            col(fwd, 2), col(bwd, 3),
            col(fwd, 4), col(bwd, 4),
            lb_spec(0), lb_spec(1),
            pl.BlockSpec((2, n_rows, 2 * c), lambda b, s: (0, 0, 0)),
        ],
        out_specs=[pl.BlockSpec((c, hw), lambda b, s: (fwd(b, s), 0)),
                   pl.BlockSpec((c, hw), lambda b, s: (bwd(b, s), 0))],
        out_shape=[jax.ShapeDtypeStruct((ntot, hw), BF16)] * 2,
        scratch_shapes=[pltpu.VMEM((2, n_heads, HEAD_DIM, HEAD_DIM), F32), pltpu.VMEM((2, n_rows, hw), F32),
                        pltpu.VMEM((2, n_heads, c, c), BF16)],
        compiler_params=_cparams(2),
    )(z, z, z, z, z, z, lb, lb, mats)


def _pool_constants(tile, n_ctx):
    assert n_ctx == tile and tile % GRID_W == 0
    band = np.zeros((2, len(POOL_WINDOWS), tile, tile), np.float32)
    inv = np.zeros((2, len(POOL_WINDOWS), tile, 1), np.float32)
    for kind, row_len in enumerate((n_ctx, GRID_W)):
        t = np.arange(tile)
        tau = t % row_len
        base = t - tau
        for g, w in enumerate(POOL_WINDOWS):
            lo = np.maximum(tau - w // 2, 0)
            hi = np.minimum(tau + w // 2 - 1, row_len - 1)
            s = np.arange(tile)[None, :]
            band[kind, g] = (s >= (base + lo)[:, None]) & (s <= (base + hi)[:, None])
            inv[kind, g, :, 0] = 1.0 / (hi - lo + 1)
    return band, inv


def _route(sel, sc):
    n_g, per = N_EXPERT_GROUPS, EXPERTS_PER_GROUP
    row = lambda a, e: a[e:e + 1, :]
    best = None
    for g in range(n_g):
        rows = [row(sel, g * per + i) for i in range(per)]
        gs = None
        for i in range(per):
            for j in range(i + 1, per):
                p = rows[i] + rows[j]
                gs = p if gs is None else jnp.maximum(gs, p)
        if best is None:
            best, g_idx = gs, jnp.zeros(gs.shape, jnp.int32)
        else:
            upd = gs > best
            g_idx = jnp.where(upd, g, g_idx)
            best = jnp.where(upd, gs, best)
    cand, raw = [], []
    for i in range(per):
        ci, ri = row(sel, i), row(sc, i)
        for g in range(1, n_g):
            ci = jnp.where(g_idx == g, row(sel, g * per + i), ci)
            ri = jnp.where(g_idx == g, row(sc, g * per + i), ri)
        cand.append(ci)
        raw.append(ri)

    def argmax4(vals):
        m, loc = vals[0], jnp.zeros(vals[0].shape, jnp.int32)
        for i in range(1, per):
            upd = vals[i] > m
            loc = jnp.where(upd, i, loc)
            m = jnp.where(upd, vals[i], m)
        return loc

    loc0 = argmax4(cand)
    loc1 = argmax4([jnp.where(loc0 == i, -jnp.inf, cand[i]) for i in range(per)])

    def pick(loc):
        out = raw[0]
        for i in range(1, per):
            out = jnp.where(loc == i, raw[i], out)
        return out

    g0, g1 = pick(loc0), pick(loc1)
    tot = g0 + g1
    return g_idx * per + loc0, g_idx * per + loc1, g0 / tot, g1 / tot


def _merge_kernel(d, hw, tiles_per_sample, ctx_row,
                  x_ref, p_ref, og_ref, gp_ref, gh_ref, of_ref, ob_ref, mod_ref, n2_ref, ps_ref, hn_ref,
                  band_ref, inv_ref, wpg_ref, wbp_ref, wbh_ref, wout_ref, wr_ref, rb_ref, tri_ref,
                  xo_ref, h2_ref, eidx_ref, gate_ref, cnt_ref, pool_scr, hg_scr, carry_ref):
    i = pl.program_id(0)
    n_heads = hw // HEAD_DIM
    n_groups = len(POOL_WINDOWS)
    pg = p_ref.shape[1] // n_groups
    mrow = _mod_row(i, tiles_per_sample, ctx_row)
    mod = lambda k: mod_ref[pl.ds(mrow, 1), k * d:(k + 1) * d]

    for h in range(n_heads):
        cs = slice(h * HEAD_DIM, (h + 1) * HEAD_DIM)
        o = of_ref[:, cs].astype(F32) + ob_ref[:, cs].astype(F32)
        o = o * lax.rsqrt(jnp.mean(o * o, axis=-1, keepdims=True) + EPS)
        og = og_ref[:, cs].astype(F32)
        hg_scr[:, cs] = (o * hn_ref[:, cs] * (og * _sigmoid(og))).astype(BF16)

    for g in range(n_groups):
        cs = slice(g * pg, (g + 1) * pg)
        u = p_ref[:, cs]
        wsum = _dot(band_ref[0, g], u)
        dlt = wsum * inv_ref[0, g] - u.astype(F32)
        pool_scr[:, cs] = (_dot(dlt.astype(BF16), wpg_ref[g]) * ps_ref[:, cs]).astype(BF16)

    gp = gp_ref[...].astype(F32)
    gh = gh_ref[...].astype(F32)
    m = _sigmoid(gp) * _dot(pool_scr[...], wbp_ref[...]) + _sigmoid(gh) * _dot(hg_scr[...], wbh_ref[...])
    y = _dot(m.astype(BF16), wout_ref[...])
    xn = x_ref[...] + mod(2) * y
    xo_ref[...] = xn
    n2 = xn * lax.rsqrt(jnp.mean(xn * xn, axis=-1, keepdims=True) + EPS) * n2_ref[...]
    h2 = n2 * (1.0 + mod(4)) + mod(3)
    h2_ref[...] = _pack_halves(h2)

    h_hi = h2.astype(BF16)
    h_lo = (h2 - h_hi.astype(F32)).astype(BF16)
    wr = wr_ref[...]
    w_hi = wr.astype(BF16)
    w_lo = (wr - w_hi.astype(F32)).astype(BF16)
    logits = _dot_nt(w_hi, h_hi) + (_dot_nt(w_hi, h_lo) + _dot_nt(w_lo, h_hi))
    sc = _sigmoid(logits)
    e0, e1, g0, g1 = _route(sc + rb_ref[...], sc)

    @pl.when(i == 0)
    def _():
        carry_ref[...] = jnp.zeros_like(carry_ref)

    er = lax.broadcasted_iota(jnp.int32, sc.shape, 0)
    oh0 = (er == e0).astype(F32)
    oh1 = (er == e1).astype(F32)
    both = oh0 + oh1
    base = carry_ref[:, 0:1] + _dot(both.astype(BF16), tri_ref[...])
    r0 = jnp.sum(oh0 * base, axis=0, keepdims=True)
    r1 = jnp.sum(oh1 * base, axis=0, keepdims=True)
    carry_ref[...] = carry_ref[...] + jnp.sum(both, axis=1, keepdims=True)
    cnt_ref[...] = carry_ref[...].astype(jnp.int32)

    eidx_ref[...] = jnp.zeros_like(eidx_ref)
    gate_ref[...] = jnp.zeros_like(gate_ref)
    eidx_ref[0, 0:1, :] = e0
    eidx_ref[0, 1:2, :] = e1
    eidx_ref[0, 2:3, :] = r0.astype(jnp.int32)
    eidx_ref[0, 3:4, :] = r1.astype(jnp.int32)
    gate_ref[0, 0:1, :] = g0
    gate_ref[0, 1:2, :] = g1


def _const_spec(shape):
    nd = len(shape)
    return pl.BlockSpec(shape, lambda i: (0,) * nd, pipeline_mode=pl.Buffered(1))


def _merge(x, z, o_f, o_b, mods_l, norm2_w, pool_scale, hg_norm_w, w_pg, w_bp, w_bh, w_out, w_router_t,
           router_bias, n_ctx, t_sample, ctx_row):
    ntot, d = x.shape
    hw = o_f.shape[1]
    d_pool = w_bp.shape[0]
    assert d_pool == hw and d == 2 * hw
    tm = ROW_TILE
    tps = t_sample // tm
    n_e = w_router_t.shape[0]
    band, inv = _pool_constants(tm, n_ctx)
    band = jnp.asarray(band, BF16)
    inv = jnp.asarray(inv, F32)
    kind = lambda i: jnp.where(i % tps == 0, 0, 1)
    n_tiles = ntot // tm
    before = jnp.asarray(np.triu(np.ones((tm, tm), np.float32), 1), BF16)
    return pl.pallas_call(
        functools.partial(_merge_kernel, d, hw, tps, ctx_row),
        name="merge",
        grid=(n_tiles,),
        in_specs=[
            pl.BlockSpec((tm, d), lambda i: (i, 0)),
            pl.BlockSpec((tm, hw), lambda i: (i, 0)),
            pl.BlockSpec((tm, hw), lambda i: (i, 5)),
            pl.BlockSpec((tm, d), lambda i: (i, 3)),
            pl.BlockSpec((tm, d), lambda i: (i, 4)),
            pl.BlockSpec((tm, hw), lambda i: (i, 0)),
            pl.BlockSpec((tm, hw), lambda i: (i, 0)),
            _const_spec((8, 6 * d)),
            _const_spec((1, d)),
            _const_spec((1, hw)),
            _const_spec((1, hw)),
            pl.BlockSpec((1,) + band.shape[1:], lambda i: (kind(i), 0, 0, 0)),
            pl.BlockSpec((1,) + inv.shape[1:], lambda i: (kind(i), 0, 0, 0)),
            _const_spec(w_pg.shape),
            _const_spec(w_bp.shape),
            _const_spec(w_bh.shape),
            _const_spec(w_out.shape),
            _const_spec(w_router_t.shape),
            _const_spec((n_e, 1)),
            _const_spec((tm, tm)),
        ],
        out_specs=[
            pl.BlockSpec((tm, d), lambda i: (i, 0)),
            pl.BlockSpec((tm, d // 2), lambda i: (i, 0)),
            pl.BlockSpec((1, 8, tm), lambda i: (i, 0, 0)),
            pl.BlockSpec((1, 8, tm), lambda i: (i, 0, 0)),
            pl.BlockSpec((n_e, 128), lambda i: (0, 0)),
        ],
        out_shape=[
            jax.ShapeDtypeStruct((ntot, d), F32),
            jax.ShapeDtypeStruct((ntot, d // 2), jnp.uint32),
            jax.ShapeDtypeStruct((n_tiles, 8, tm), jnp.int32),
            jax.ShapeDtypeStruct((n_tiles, 8, tm), F32),
            jax.ShapeDtypeStruct((n_e, 128), jnp.int32),
        ],
        scratch_shapes=[pltpu.VMEM((tm, hw), BF16), pltpu.VMEM((tm, hw), BF16), pltpu.VMEM((n_e, 128), F32)],
        compiler_params=_cparams(1),
    )(x, z, z, z, z, o_f, o_b, mods_l, norm2_w.reshape(1, d), pool_scale.reshape(1, hw),
      hg_norm_w.reshape(1, hw), band, inv, w_pg, w_bp, w_bh, w_out, w_router_t,
      router_bias.astype(F32).reshape(n_e, 1), before)


def _row_gather_start(src_hbm, idx_ref, dst, sem, n_rows):
    def body(it, carry):
        for k in range(2):
            r = 2 * it + k
            tok = idx_ref[0, 0, r]
            pltpu.make_async_copy(src_hbm.at[pl.ds(tok, 1)], dst.at[pl.ds(r, 1)], sem).start(priority=k)
        return carry

    lax.fori_loop(0, n_rows // 2, body, 0, unroll=4)


def _row_gather_wait(src_hbm, dst, sem, n_rows):
    pltpu.make_async_copy(src_hbm.at[pl.ds(0, n_rows)], dst, sem).wait()


def _dispatch_kernel(n_e, n_blk, tails_ref, pos_ref, h_ref, xs_hbm, zblk, sems):
    tm = h_ref.shape[0]
    rows = zblk.shape[0]
    sem = sems.at[0]
    zsem = sems.at[1]

    def for_each_pad(act):
        for e in range(n_e):
            def body(r, carry, e=e):
                act(pltpu.make_async_copy(zblk.at[pl.ds(0, 1)], xs_hbm.at[pl.ds(tails_ref[e] + r, 1)], zsem))
                return carry

            lax.fori_loop(0, tails_ref[n_e + e], body, 0)

        def blk_body(b, carry):
            act(pltpu.make_async_copy(zblk, xs_hbm.at[pl.ds(pl.multiple_of(b * rows, rows), rows)], zsem))
            return carry

        lax.fori_loop(tails_ref[2 * n_e], n_blk, blk_body, 0)

    @pl.when(pl.program_id(0) == 0)
    def _():
        zblk[...] = jnp.zeros_like(zblk)
        for_each_pad(lambda cp: cp.start())

    def body(r, carry):
        for k in range(2):
            dst = pos_ref[0, 0, k * tm + r]
            pltpu.make_async_copy(h_ref.at[pl.ds(r, 1)], xs_hbm.at[pl.ds(dst, 1)], sem).start(priority=k)
        return carry

    lax.fori_loop(0, tm, body, 0, unroll=8)
    pltpu.make_async_copy(xs_hbm.at[pl.ds(0, 2 * tm)], xs_hbm.at[pl.ds(0, 2 * tm)], sem).wait()

    @pl.when(pl.program_id(0) == 0)
    def _():
        for_each_pad(lambda cp: cp.wait())


def _dispatch(h2, pos, tails, n_blk):
    ntot, d = h2.shape
    tm = ROW_TILE
    n_tiles = ntot // tm
    n_e = (tails.shape[0] - 1) // 2
    grid_spec = pltpu.PrefetchScalarGridSpec(
        num_scalar_prefetch=1,
        grid=(n_tiles,),
        in_specs=[
            pl.BlockSpec((1, 1, 2 * tm), lambda i, t: (i, 0, 0), memory_space=pltpu.SMEM),
            pl.BlockSpec((tm, d), lambda i, t: (i, 0)),
        ],
        out_specs=pl.BlockSpec(memory_space=pl.ANY),
        scratch_shapes=[pltpu.VMEM((MOE_ROWS, d), h2.dtype), pltpu.SemaphoreType.DMA((2,))],
    )
    return pl.pallas_call(
        functools.partial(_dispatch_kernel, n_e, n_blk),
        name="dispatch",
        grid_spec=grid_spec,
        out_shape=jax.ShapeDtypeStruct((n_blk * MOE_ROWS, d), h2.dtype),
        compiler_params=_cparams(1),
    )(tails, pos, h2)


def _expert_kernel(blk_e_ref, nused_ref, x_ref, w1_ref, w3_ref, w2_ref, y_ref, w1c, w3c, w2c):
    i = pl.program_id(0)
    n_used = nused_ref[0]

    @pl.when(i < n_used)
    def _():
        changed = jnp.logical_or(i == 0, blk_e_ref[i] != blk_e_ref[jnp.maximum(i - 1, 0)])

        @pl.when(changed)
        def _():
            w1c[...] = w1_ref[0, 0].astype(BF16)
            w3c[...] = w3_ref[0, 0].astype(BF16)
            w2c[...] = w2_ref[0, 0].astype(BF16)

        xb = jnp.concatenate(_unpack_halves(x_ref[...]), axis=1).astype(BF16)
        h1 = _dot(xb, w1c[...])
        h3 = _dot(xb, w3c[...])
        act = (h1 * _sigmoid(h1) * h3).astype(BF16)
        y_ref[...] = _pack_halves(_dot(act, w2c[...]))

    @pl.when(i >= n_used)
    def _():
        y_ref[...] = jnp.zeros_like(y_ref)


def _experts(xs, blk_e, n_used, w1, w3, w2, layer):
    n_rows = xs.shape[0]
    _, n_e, d, f = w1.shape
    n_blk = blk_e.shape[0]
    rows = MOE_ROWS
    used = lambda i, nu: jnp.minimum(i, nu[0] - 1)
    grid_spec = pltpu.PrefetchScalarGridSpec(
        num_scalar_prefetch=2,
        grid=(n_blk,),
        in_specs=[
            pl.BlockSpec((rows, d // 2), lambda i, be, nu: (used(i, nu), 0)),
            pl.BlockSpec((1, 1, d, f), lambda i, be, nu: (layer, be[i], 0, 0)),
            pl.BlockSpec((1, 1, d, f), lambda i, be, nu: (layer, be[i], 0, 0)),
            pl.BlockSpec((1, 1, f, d), lambda i, be, nu: (layer, be[i], 0, 0)),
        ],
        out_specs=pl.BlockSpec((rows, d // 2), lambda i, be, nu: (i, 0)),
        scratch_shapes=[
            pltpu.VMEM((d, f), BF16),
            pltpu.VMEM((d, f), BF16),
            pltpu.VMEM((f, d), BF16),
        ],
    )
    return pl.pallas_call(
        _expert_kernel,
        name="experts",
        grid_spec=grid_spec,
        out_shape=jax.ShapeDtypeStruct((n_rows, d // 2), jnp.uint32),
        compiler_params=_cparams(1),
    )(blk_e, n_used, xs, w1, w3, w2)


def _combine_kernel(d, mode, tile_of, pos0_ref, posn_ref, y_hbm, x_ref, g_ref, mod_ref, *rest):
    if mode == "next":
        modn_ref, nw_ref, xo_ref, h_ref, rbuf, sem = rest
    else:
        fw_ref, out_ref, rbuf, sem = rest
    i = pl.program_id(0)
    n = pl.num_programs(0)
    tm = x_ref.shape[0]
    slot = i % 2

    @pl.when(i == 0)
    def _():
        _row_gather_start(y_hbm, pos0_ref, rbuf.at[0], sem.at[0], 2 * tm)

    @pl.when(i + 1 < n)
    def _():
        _row_gather_start(y_hbm, posn_ref, rbuf.at[1 - slot], sem.at[1 - slot], 2 * tm)

    _row_gather_wait(y_hbm, rbuf.at[slot], sem.at[slot], 2 * tm)
    mrow = tile_of(i)[1]
    m5 = mod_ref[pl.ds(mrow, 1), 5 * d:6 * d]
    g = g_ref[...]
    lo0, hi0 = _unpack_halves(rbuf[slot, 0:tm, :])
    lo1, hi1 = _unpack_halves(rbuf[slot, tm:2 * tm, :])
    g0, g1 = g[:, 0:1], g[:, 1:2]
    f = jnp.concatenate([g0 * lo0 + g1 * lo1, g0 * hi0 + g1 * hi1], axis=1)
    xn = x_ref[...] + m5 * f
    if mode == "next":
        xo_ref[...] = xn
        shift = modn_ref[pl.ds(mrow, 1), 0:d]
        scale = modn_ref[pl.ds(mrow, 1), d:2 * d]
        h_ref[...] = _norm_mod(xn, nw_ref[...], shift, scale).astype(BF16)
    else:
        out_ref[...] = xn * lax.rsqrt(jnp.mean(xn * xn, axis=-1, keepdims=True) + EPS) * fw_ref[...]


def _combine(x, yb, pos, gates, mods_l, n_batch, n_ctx, n_lat, mods_next=None, norm_next=None, final_w=None):
    ntot, d = x.shape
    tm = ROW_TILE
    tps, cps, lps = (n_ctx + n_lat) // tm, n_ctx // tm, n_lat // tm
    mode = "final" if final_w is not None else "next"
    if mode == "final":
        n_steps = n_batch * lps

        def tile_of(i):
            return (i // lps) * tps + cps + i % lps, i // lps
    else:
        n_steps = ntot // tm

        def tile_of(i):
            return i, _mod_row(i, tps, n_batch)

    tile = lambda i: tile_of(i)[0]
    in_specs = [
        pl.BlockSpec((1, 1, 2 * tm), lambda i: (tile(0), 0, 0), memory_space=pltpu.SMEM),
        pl.BlockSpec((1, 1, 2 * tm), lambda i: (tile(jnp.minimum(i + 1, n_steps - 1)), 0, 0),
                     memory_space=pltpu.SMEM),
        pl.BlockSpec(memory_space=pl.ANY),
        pl.BlockSpec((tm, d), lambda i: (tile(i), 0)),
        pl.BlockSpec((tm, 2), lambda i: (tile(i), 0)),
        _const_spec((8, 6 * d)),
    ]
    args = [pos, pos, yb, x, gates, mods_l]
    if mode == "next":
        in_specs += [pl.BlockSpec((8, 2 * d), lambda i: (0, 0)), pl.BlockSpec((1, d), lambda i: (0, 0))]
        args += [mods_next, norm_next.reshape(1, d)]
        out_specs = [pl.BlockSpec((tm, d), lambda i: (i, 0))] * 2
        out_shape = [jax.ShapeDtypeStruct((ntot, d), F32), jax.ShapeDtypeStruct((ntot, d), BF16)]
    else:
        in_specs += [pl.BlockSpec((1, d), lambda i: (0, 0))]
        args += [final_w.reshape(1, d)]
        out_specs = pl.BlockSpec((tm, d), lambda i: (i, 0))
        out_shape = jax.ShapeDtypeStruct((n_steps * tm, d), F32)
    return pl.pallas_call(
        functools.partial(_combine_kernel, d, mode, tile_of),
        name="combine",
        grid=(n_steps,),
        in_specs=in_specs,
        out_specs=out_specs,
        out_shape=out_shape,
        scratch_shapes=[pltpu.VMEM((2, 2 * tm, d // 2), jnp.uint32), pltpu.SemaphoreType.DMA((2,))],
        compiler_params=_cparams(1),
    )(*args)


def _dispatch_plan(eidx, counts, n_e):
    rows = MOE_ROWS
    n_tiles, _, tm = eidx.shape
    n_assign = n_tiles * tm * 2
    pcounts = (counts + rows - 1) // rows * rows
    pends = jnp.cumsum(pcounts)
    pstart = pends - pcounts
    experts = jnp.arange(n_e, dtype=jnp.int32)
    first = jnp.sum(jnp.where(eidx[:, 0:2, :, None] == experts, pstart, 0), axis=-1)
    dest = first + eidx[:, 2:4, :]
    n_blk = -(-n_assign // rows) + n_e
    blk_row0 = jnp.arange(n_blk, dtype=jnp.int32) * rows
    blk_e = jnp.minimum(jnp.sum(blk_row0[:, None] >= pends[None, :], axis=1), n_e - 1).astype(jnp.int32)
    n_used = (pends[-1] // rows).astype(jnp.int32).reshape(1)
    pos = dest.astype(jnp.int32).reshape(n_tiles, 1, 2 * tm)
    tails = jnp.concatenate([pstart + counts, pcounts - counts, n_used]).astype(jnp.int32)
    return blk_e, n_used, pos, n_blk, tails


def kernel(x, c, ctx, c_ctx, w_ada, b_ada, norm1_w, norm2_w, w_in, w_pool_group, pool_scale, hg_norm_w,
           lb_logits, w_branch_pool, w_branch_hgrn, w_out, w_router, router_bias, w_e1, w_e3, w_e2,
           final_norm_w):
    n_batch, n_lat, d = x.shape
    n_ctx = ctx.shape[1]
    depth = w_ada.shape[0]
    hw = hg_norm_w.shape[1]
    n_e = w_router.shape[1]
    t_sample = n_ctx + n_lat
    ntot = n_batch * t_sample
    ctx_row = n_batch
    assert n_batch < 8 and n_ctx == ROW_TILE and n_lat % ROW_TILE == 0

    cc = jnp.zeros((8, d), F32).at[:n_batch].set(c).at[ctx_row].set(c_ctx)
    mods = _ada_mods(cc, w_ada, b_ada)
    lb = _lower_bounds(lb_logits)
    w_router_t = jnp.transpose(w_router.astype(F32))

    xs, h = _norm_mod_rows(x, ctx, mods[0], norm1_w[0])
    for l in range(depth):
        z = _in_proj(h, w_in, l)
        o_f, o_b = _scan(z, lb, l, depth, n_batch, n_ctx, n_lat, hw)
        xs, h2, eidx, gate, counts = _merge(
            xs, z, o_f, o_b, mods[l], norm2_w[l], pool_scale[l], hg_norm_w[l], w_pool_group[l].astype(BF16),
            w_branch_pool[l].astype(BF16), w_branch_hgrn[l].astype(BF16), w_out[l].astype(BF16),
            w_router_t, router_bias, n_ctx, t_sample, ctx_row)
        blk_e, n_used, pos, n_blk, tails = _dispatch_plan(eidx, counts[:, 0], n_e)
        yb = _experts(_dispatch(h2, pos, tails, n_blk), blk_e, n_used, w_e1, w_e3, w_e2, l)
        g = jnp.transpose(gate[:, 0:2, :], (0, 2, 1)).reshape(ntot, 2)
        if l + 1 < depth:
            xs, h = _combine(xs, yb, pos, g, mods[l], n_batch, n_ctx, n_lat,
                             mods_next=mods[l + 1], norm_next=norm1_w[l + 1])
        else:
            out = _combine(xs, yb, pos, g, mods[l], n_batch, n_ctx, n_lat, final_w=final_norm_w)
    return out.reshape(n_batch, n_lat, d)
```

```python
import functools
import math

import numpy as np
import jax
import jax.numpy as jnp
from jax import lax
from jax.experimental import pallas as pl
from jax.experimental.pallas import tpu as pltpu

F32 = jnp.float32
BF16 = jnp.bfloat16

EPS = 1e-6
GRID_W = 64
POOL_WINDOWS = (2, 4, 8, 16)
HEAD_DIM = 128
N_EXPERT_GROUPS = 4
EXPERTS_PER_GROUP = 4
SCAN_CHUNK = 64
SCAN_GROUP = 2
ROW_TILE = 256
MOE_ROWS = 256
V7X_VMEM_BYTES = 64 * 1024 * 1024
VMEM_LIMIT = V7X_VMEM_BYTES - 8 * 1024 * 1024


def _cparams(n_axes, vmem=VMEM_LIMIT):
    return pltpu.CompilerParams(dimension_semantics=("arbitrary",) * n_axes, vmem_limit_bytes=vmem)


def _sigmoid(x):
    return 1.0 / (1.0 + jnp.exp(-x))


def _dot(a, b):
    return jnp.dot(a, b, preferred_element_type=F32)


def _dot_nt(a, b):
    return lax.dot_general(a, b, (((1,), (1,)), ((), ())), preferred_element_type=F32)


def _pack_halves(x):
    half = x.shape[1] // 2
    lo = lax.bitcast_convert_type(x[:, :half].astype(BF16).astype(F32), jnp.uint32)
    hi = lax.bitcast_convert_type(x[:, half:].astype(BF16).astype(F32), jnp.uint32)
    return (lo >> 16) | (hi & jnp.uint32(0xFFFF0000))


def _unpack_halves(w):
    lo = lax.bitcast_convert_type(w << 16, F32)
    hi = lax.bitcast_convert_type(w & jnp.uint32(0xFFFF0000), F32)
    return lo, hi


def _lb_kernel(depth, logit_ref, lb_ref):
    for d in range(2):
        rows = [logit_ref[d * depth + i:d * depth + i + 1, :] for i in range(depth)]
        m = rows[0]
        for r in rows[1:]:
            m = jnp.maximum(m, r)
        es = [jnp.exp(r - m) for r in rows]
        tot = es[0]
        for e in es[1:]:
            tot = tot + e
        ws = [e / tot for e in es]
        acc = jnp.zeros_like(ws[0])
        for i in range(depth):
            acc = acc + ws[i]
            lb_ref[d * depth + i:d * depth + i + 1, :] = acc - ws[0]


def _lower_bounds(lb_logits):
    _, depth, hw = lb_logits.shape
    flat = lb_logits.astype(F32).reshape(2 * depth, hw)
    out = pl.pallas_call(
        functools.partial(_lb_kernel, depth),
        name="lower_bounds",
        out_shape=jax.ShapeDtypeStruct((2 * depth, hw), F32),
    )(flat)
    return out.reshape(2 * depth, 1, hw)


def _ada_kernel(c_ref, w_ref, b_ref, o_ref):
    c = c_ref[...]
    s = (c * _sigmoid(c)).astype(BF16)
    o_ref[0] = _dot(s, w_ref[0].astype(BF16)) + b_ref[0]


def _ada_mods(cc, w_ada, b_ada):
    depth, d, n6 = w_ada.shape
    tn = 1024
    return pl.pallas_call(
        _ada_kernel,
        name="ada_mods",
        grid=(depth, n6 // tn),
        in_specs=[
            pl.BlockSpec((8, d), lambda l, j: (0, 0)),
            pl.BlockSpec((1, d, tn), lambda l, j: (l, 0, j)),
            pl.BlockSpec((1, 1, tn), lambda l, j: (l, 0, j)),
        ],
        out_specs=pl.BlockSpec((1, 8, tn), lambda l, j: (l, 0, j)),
        out_shape=jax.ShapeDtypeStruct((depth, 8, n6), F32),
        compiler_params=_cparams(2),
    )(cc, w_ada, b_ada.reshape(depth, 1, n6))


INPROJ_MAX_ROWS = 2176
INPROJ_COLS = 1024


def _mod_row(i, tiles_per_sample, ctx_row):
    return jnp.where(i % tiles_per_sample == 0, ctx_row, i // tiles_per_sample)


def _norm_mod(x, nw, shift, scale):
    n = x * lax.rsqrt(jnp.mean(x * x, axis=-1, keepdims=True) + EPS) * nw
    return n * (1.0 + scale) + shift


def _normmod_kernel(d, tiles_per_sample, ctx_row, x_ref, c_ref, mod_ref, nw_ref, xs_ref, h_ref):
    i = pl.program_id(0)
    mrow = _mod_row(i, tiles_per_sample, ctx_row)
    shift = mod_ref[pl.ds(mrow, 1), 0:d]
    scale = mod_ref[pl.ds(mrow, 1), d:2 * d]

    def emit(src_ref):
        x = src_ref[...]
        xs_ref[...] = x
        h_ref[...] = _norm_mod(x, nw_ref[...], shift, scale).astype(BF16)

    is_ctx = i % tiles_per_sample == 0
    pl.when(is_ctx)(lambda: emit(c_ref))
    pl.when(jnp.logical_not(is_ctx))(lambda: emit(x_ref))


def _norm_mod_rows(x, ctx, mods_l, norm_w):
    n_batch, n_lat, d = x.shape
    n_ctx = ctx.shape[1]
    tm = ROW_TILE
    assert n_ctx == tm
    lps = n_lat // tm
    tps = lps + 1
    ntot = n_batch * (n_ctx + n_lat)
    return pl.pallas_call(
        functools.partial(_normmod_kernel, d, tps, n_batch),
        name="norm_mod",
        grid=(n_batch * tps,),
        in_specs=[
            pl.BlockSpec((tm, d), lambda i: ((i // tps) * lps + jnp.maximum(i % tps - 1, 0), 0)),
            pl.BlockSpec((tm, d), lambda i: (i // tps, 0)),
            pl.BlockSpec((8, 2 * d), lambda i: (0, 0)),
            pl.BlockSpec((1, d), lambda i: (0, 0)),
        ],
        out_specs=[pl.BlockSpec((tm, d), lambda i: (i, 0))] * 2,
        out_shape=[jax.ShapeDtypeStruct((ntot, d), F32), jax.ShapeDtypeStruct((ntot, d), BF16)],
        compiler_params=_cparams(1),
    )(x.reshape(n_batch * n_lat, d), ctx.reshape(n_batch * n_ctx, d), mods_l, norm_w.reshape(1, d))


def _inproj_kernel(h_ref, w_ref, o_ref):
    o_ref[...] = _dot(h_ref[...], w_ref[0].astype(BF16)).astype(BF16)


def _in_proj(h, w_in, layer):
    ntot, d = h.shape
    n_cols = w_in.shape[2]
    tn = INPROJ_COLS
    tm = max(k for k in range(16, INPROJ_MAX_ROWS + 1, 16) if ntot % k == 0)
    return pl.pallas_call(
        _inproj_kernel,
        name="in_proj",
        grid=(ntot // tm, n_cols // tn),
        in_specs=[
            pl.BlockSpec((tm, d), lambda r, j: (r, 0)),
            pl.BlockSpec((1, d, tn), lambda r, j: (layer, 0, j)),
        ],
        out_specs=pl.BlockSpec((tm, tn), lambda r, j: (r, j)),
        out_shape=jax.ShapeDtypeStruct((ntot, n_cols), BF16),
        compiler_params=_cparams(2),
    )(h, w_in)


COMPACT_MIN_HALF = 8
MXU_LEVEL_MIN_HALF = 2


def _scan_levels(c):
    out, m = [], c // 2
    while m >= MXU_LEVEL_MIN_HALF:
        out.append(m)
        m //= 2
    return out


def _scan_row_layout(c):
    off = {"b": 0, "kend": c, "tot": 2 * c}
    pos = 2 * c + 8
    for m in _scan_levels(c):
        n = c // 2 if m >= COMPACT_MIN_HALF else c
        off[("q", m)] = pos
        off[("k", m)] = pos + n
        pos += 2 * n
    return off, pos


def _scan_constants(c):
    t = np.arange(c)[:, None]
    r = np.arange(c)[None, :]
    out = []
    for flip in (False, True):
        fl = (lambda a: a[::-1, ::-1]) if flip else (lambda a: a)
        sets = [fl(r <= t), fl(r > t), np.ones((8, c), bool)]
        for m in _scan_levels(c):
            mid = (t // (2 * m)) * (2 * m) + m
            qa = fl((t >= mid) & (r >= mid) & (r <= t))
            ka = fl((t < mid) & (r > t) & (r <= mid - 1))
            if m >= COMPACT_MIN_HALF:
                rows = np.arange(c)
                q_valid = (rows % (2 * m) >= m) != flip
                sets += [qa[q_valid], ka[~q_valid]]
            else:
                sets += [qa, ka]
        mat = np.concatenate(sets, axis=0).astype(np.float32)
        out.append(np.concatenate([mat, mat], axis=1))
    return np.stack(out)


def _tile_pairs(e, m, c):
    parts = []
    for p in range(c // (2 * m)):
        blk = e[p * m:(p + 1) * m]
        parts += [blk, blk]
    return jnp.concatenate(parts, axis=0)


def _scan_gates(c, q_ref, f_ref, lb_ref, mat, e_ref):
    zq = q_ref[...].astype(F32)
    zf = f_ref[...].astype(F32)
    lbv = lb_ref[0]
    f = lbv + (1.0 - lbv) * _sigmoid(zf)
    kk = 1.0 - f
    qq = zq * _sigmoid(zq)
    lf2 = jnp.log(f) * math.log2(math.e)
    lf_hi = lf2.astype(BF16)
    lf_lo = (lf2 - lf_hi.astype(F32)).astype(BF16)
    e_ref[...] = jnp.exp2(_dot(mat, jnp.concatenate([lf_hi, lf_lo], axis=0)))
    row = lax.broadcasted_iota(jnp.int32, (c, 1), 0)
    k_sw = jnp.where((row & 1) == 1, pltpu.roll(kk, 1, axis=0), pltpu.roll(kk, c - 1, axis=0))
    return qq, kk, qq * f, k_sw


def _scan_intra(n_heads, c, backward, qq, kk, qf, k_sw, e_ref, a_ref):
    levels = _scan_levels(c)
    off, _ = _scan_row_layout(c)
    ti = lax.broadcasted_iota(jnp.int32, (c, c), 0)
    si = lax.broadcasted_iota(jnp.int32, (c, c), 1)
    same = {m: (ti & -(2 * m)) == (si & -(2 * m)) for m in levels[1:] + [1]}
    eye = ti == si
    causal = (ti <= si) if backward else (ti >= si)
    for h in range(n_heads):
        cs = slice(h * HEAD_DIM, (h + 1) * HEAD_DIM)
        qh = qq[:, cs]
        kh = kk[:, cs]
        a = None
        for m in levels:
            n = c // 2 if m >= COMPACT_MIN_HALF else c
            eq = e_ref[off[("q", m)]:off[("q", m)] + n, cs]
            ek = e_ref[off[("k", m)]:off[("k", m)] + n, cs]
            if m >= COMPACT_MIN_HALF:
                eq, ek = _tile_pairs(eq, m, c), _tile_pairs(ek, m, c)
            p = _dot_nt((qh * eq).astype(BF16), (kh * ek).astype(BF16))
            a = p if a is None else jnp.where(same[m], p, a)
        pair = jnp.sum(qf[:, cs] * k_sw[:, cs], axis=-1, keepdims=True)
        diag = jnp.sum(qh * kh, axis=-1, keepdims=True)
        a = jnp.where(same[1], pair, a)
        a = jnp.where(eye, diag, a)
        a_ref[h] = jnp.where(causal, a, 0.0).astype(BF16)


def _scan_state(n_heads, c, qq, kk, i_ref, e_ref, a_ref, s_ref, o_ref):
    off, _ = _scan_row_layout(c)
    for h in range(n_heads):
        cs = slice(h * HEAD_DIM, (h + 1) * HEAD_DIM)
        qh = qq[:, cs]
        kh = kk[:, cs]
        vh = i_ref[:, cs]
        st = s_ref[h]
        qb = (qh * e_ref[off["b"]:off["b"] + c, cs]).astype(BF16)
        o = _dot_nt(qb, st.astype(BF16)) + _dot(a_ref[h], vh)
        o_ref[:, cs] = o.astype(BF16)
        vt = vh.astype(F32).T.astype(BF16)
        kt = (kh * e_ref[off["kend"]:off["kend"] + c, cs]).astype(BF16)
        s_ref[h] = st * e_ref[off["tot"]:off["tot"] + 1, cs] + _dot(vt, kt)


def _scan_kernel(n_heads, c, qf_ref, qb_ref, ff_ref, fb_ref, if_ref, ib_ref, lbf_ref, lbb_ref, m_ref,
                 of_ref, ob_ref, s_ref, e_ref, a_ref):
    @pl.when(pl.program_id(1) == 0)
    def _():
        s_ref[...] = jnp.zeros_like(s_ref)

    dirs = ((qf_ref, ff_ref, if_ref, lbf_ref, of_ref), (qb_ref, fb_ref, ib_ref, lbb_ref, ob_ref))
    work = [(d, pl.ds((SCAN_GROUP - 1 - k if d else k) * c, c)) for d in range(2) for k in range(SCAN_GROUP)]
    gates = []
    for n, (d, rows) in enumerate(work):
        q, f, _, lb, _ = dirs[d]
        gates.append(_scan_gates(c, q.at[rows], f.at[rows], lb, m_ref[d], e_ref.at[n]))
    for n, (d, rows) in enumerate(work):
        _scan_intra(n_heads, c, d == 1, *gates[n], e_ref.at[n], a_ref.at[n])
    for n, (d, rows) in enumerate(work):
        _, _, i_ref, _, o_ref = dirs[d]
        _scan_state(n_heads, c, gates[n][0], gates[n][1], i_ref.at[rows], e_ref.at[n], a_ref.at[n],
                    s_ref.at[d], o_ref.at[rows])


def _scan(z, lb, layer, depth, n_batch, n_ctx, n_lat, hw):
    c = SCAN_CHUNK
    blk = SCAN_GROUP * c
    n_heads = hw // HEAD_DIM
    assert n_ctx % blk == 0 and n_lat % blk == 0
    nc, nl = n_ctx // blk, n_lat // blk
    steps = nc + nl
    mats = jnp.asarray(_scan_constants(c), BF16)
    n_rows = mats.shape[1]
    assert n_rows == _scan_row_layout(c)[1]
    ntot = z.shape[0]

    def fwd(b, s):
        return b * steps + s

    def bwd(b, s):
        return b * steps + jnp.where(s < nc, nc - 1 - s, nc + (steps - 1 - s))

    def col(chunk, j):
        return pl.BlockSpec((blk, hw), lambda b, s: (chunk(b, s), j))

    lb_spec = lambda d: pl.BlockSpec((1, 1, hw), lambda b, s: (d * depth + layer, 0, 0))
    return pl.pallas_call(
        functools.partial(_scan_kernel, n_heads, c),
        name="hgrn2_scan",
        grid=(n_batch, steps),
        in_specs=[
            col(fwd, 1), col(bwd, 1),
            col(fwd, 2), col(bwd, 3),
            col(fwd, 4), col(bwd, 4),
            lb_spec(0), lb_spec(1),
            pl.BlockSpec((2, n_rows, 2 * c), lambda b, s: (0, 0, 0)),
        ],
        out_specs=[pl.BlockSpec((blk, hw), lambda b, s: (fwd(b, s), 0)),
                   pl.BlockSpec((blk, hw), lambda b, s: (bwd(b, s), 0))],
        out_shape=[jax.ShapeDtypeStruct((ntot, hw), BF16)] * 2,
        scratch_shapes=[pltpu.VMEM((2, n_heads, HEAD_DIM, HEAD_DIM), F32),
                        pltpu.VMEM((2 * SCAN_GROUP, n_rows, hw), F32),
                        pltpu.VMEM((2 * SCAN_GROUP, n_heads, c, c), BF16)],
        compiler_params=_cparams(2),
    )(z, z, z, z, z, z, lb, lb, mats)


def _pool_constants(tile, n_ctx):
    assert n_ctx == tile and tile % GRID_W == 0
    band = np.zeros((2, len(POOL_WINDOWS), tile, tile), np.float32)
    inv = np.zeros((2, len(POOL_WINDOWS), tile, 1), np.float32)
    for kind, row_len in enumerate((n_ctx, GRID_W)):
        t = np.arange(tile)
        tau = t % row_len
        base = t - tau
        for g, w in enumerate(POOL_WINDOWS):
            lo = np.maximum(tau - w // 2, 0)
            hi = np.minimum(tau + w // 2 - 1, row_len - 1)
            s = np.arange(tile)[None, :]
            band[kind, g] = (s >= (base + lo)[:, None]) & (s <= (base + hi)[:, None])
            inv[kind, g, :, 0] = 1.0 / (hi - lo + 1)
    return band, inv


def _route(sel, sc):
    n_g, per = N_EXPERT_GROUPS, EXPERTS_PER_GROUP
    row = lambda a, e: a[e:e + 1, :]
    best = None
    for g in range(n_g):
        rows = [row(sel, g * per + i) for i in range(per)]
        gs = None
        for i in range(per):
            for j in range(i + 1, per):
                p = rows[i] + rows[j]
                gs = p if gs is None else jnp.maximum(gs, p)
        if best is None:
            best, g_idx = gs, jnp.zeros(gs.shape, jnp.int32)
        else:
            upd = gs > best
            g_idx = jnp.where(upd, g, g_idx)
            best = jnp.where(upd, gs, best)
    cand, raw = [], []
    for i in range(per):
        ci, ri = row(sel, i), row(sc, i)
        for g in range(1, n_g):
            ci = jnp.where(g_idx == g, row(sel, g * per + i), ci)
            ri = jnp.where(g_idx == g, row(sc, g * per + i), ri)
        cand.append(ci)
        raw.append(ri)

    def argmax4(vals):
        m, loc = vals[0], jnp.zeros(vals[0].shape, jnp.int32)
        for i in range(1, per):
            upd = vals[i] > m
            loc = jnp.where(upd, i, loc)
            m = jnp.where(upd, vals[i], m)
        return loc

    loc0 = argmax4(cand)
    loc1 = argmax4([jnp.where(loc0 == i, -jnp.inf, cand[i]) for i in range(per)])

    def pick(loc):
        out = raw[0]
        for i in range(1, per):
            out = jnp.where(loc == i, raw[i], out)
        return out

    g0, g1 = pick(loc0), pick(loc1)
    tot = g0 + g1
    return g_idx * per + loc0, g_idx * per + loc1, g0 / tot, g1 / tot


def _merge_kernel(d, hw, tiles_per_sample, ctx_row,
                  x_ref, p_ref, og_ref, gp_ref, gh_ref, of_ref, ob_ref, mod_ref, n2_ref, ps_ref, hn_ref,
                  band_ref, inv_ref, wpg_ref, wbp_ref, wbh_ref, wout_ref, wr_ref, rb_ref, tri_ref,
                  xo_ref, h2_ref, eidx_ref, gate_ref, cnt_ref, pool_scr, hg_scr, carry_ref):
    i = pl.program_id(0)
    n_heads = hw // HEAD_DIM
    n_groups = len(POOL_WINDOWS)
    pg = p_ref.shape[1] // n_groups
    mrow = _mod_row(i, tiles_per_sample, ctx_row)
    mod = lambda k: mod_ref[pl.ds(mrow, 1), k * d:(k + 1) * d]

    for h in range(n_heads):
        cs = slice(h * HEAD_DIM, (h + 1) * HEAD_DIM)
        o = of_ref[:, cs].astype(F32) + ob_ref[:, cs].astype(F32)
        o = o * lax.rsqrt(jnp.mean(o * o, axis=-1, keepdims=True) + EPS)
        og = og_ref[:, cs].astype(F32)
        hg_scr[:, cs] = (o * hn_ref[:, cs] * (og * _sigmoid(og))).astype(BF16)

    for g in range(n_groups):
        cs = slice(g * pg, (g + 1) * pg)
        u = p_ref[:, cs]
        wsum = _dot(band_ref[0, g], u)
        dlt = wsum * inv_ref[0, g] - u.astype(F32)
        pool_scr[:, cs] = (_dot(dlt.astype(BF16), wpg_ref[g]) * ps_ref[:, cs]).astype(BF16)

    gp = gp_ref[...].astype(F32)
    gh = gh_ref[...].astype(F32)
    m = _sigmoid(gp) * _dot(pool_scr[...], wbp_ref[...]) + _sigmoid(gh) * _dot(hg_scr[...], wbh_ref[...])
    y = _dot(m.astype(BF16), wout_ref[...])
    xn = x_ref[...] + mod(2) * y
    xo_ref[...] = xn
    n2 = xn * lax.rsqrt(jnp.mean(xn * xn, axis=-1, keepdims=True) + EPS) * n2_ref[...]
    h2 = n2 * (1.0 + mod(4)) + mod(3)
    h2_ref[...] = _pack_halves(h2)

    sc = _sigmoid(_dot_nt(wr_ref[...].astype(BF16), h2.astype(BF16)))
    e0, e1, g0, g1 = _route(sc + rb_ref[...], sc)

    @pl.when(i == 0)
    def _():
        carry_ref[...] = jnp.zeros_like(carry_ref)

    er = lax.broadcasted_iota(jnp.int32, sc.shape, 0)
    oh0 = (er == e0).astype(F32)
    oh1 = (er == e1).astype(F32)
    both = oh0 + oh1
    base = carry_ref[:, 0:1] + _dot(both.astype(BF16), tri_ref[...])
    r0 = jnp.sum(oh0 * base, axis=0, keepdims=True)
    r1 = jnp.sum(oh1 * base, axis=0, keepdims=True)
    carry_ref[...] = carry_ref[...] + jnp.sum(both, axis=1, keepdims=True)
    cnt_ref[...] = carry_ref[...].astype(jnp.int32)

    eidx_ref[...] = jnp.zeros_like(eidx_ref)
    gate_ref[...] = jnp.zeros_like(gate_ref)
    eidx_ref[0, 0:1, :] = e0
    eidx_ref[0, 1:2, :] = e1
    eidx_ref[0, 2:3, :] = r0.astype(jnp.int32)
    eidx_ref[0, 3:4, :] = r1.astype(jnp.int32)
    gate_ref[0, 0:1, :] = g0
    gate_ref[0, 1:2, :] = g1


def _const_spec(shape):
    nd = len(shape)
    return pl.BlockSpec(shape, lambda i: (0,) * nd, pipeline_mode=pl.Buffered(1))


def _merge(x, z, o_f, o_b, mods_l, norm2_w, pool_scale, hg_norm_w, w_pg, w_bp, w_bh, w_out, w_router_t,
           router_bias, n_ctx, t_sample, ctx_row):
    ntot, d = x.shape
    hw = o_f.shape[1]
    d_pool = w_bp.shape[0]
    assert d_pool == hw and d == 2 * hw
    tm = ROW_TILE
    tps = t_sample // tm
    n_e = w_router_t.shape[0]
    band, inv = _pool_constants(tm, n_ctx)
    band = jnp.asarray(band, BF16)
    inv = jnp.asarray(inv, F32)
    kind = lambda i: jnp.where(i % tps == 0, 0, 1)
    n_tiles = ntot // tm
    before = jnp.asarray(np.triu(np.ones((tm, tm), np.float32), 1), BF16)
    return pl.pallas_call(
        functools.partial(_merge_kernel, d, hw, tps, ctx_row),
        name="merge",
        grid=(n_tiles,),
        in_specs=[
            pl.BlockSpec((tm, d), lambda s: (s, 0)),
            pl.BlockSpec((tm, hw), lambda s: (s, 0)),
            pl.BlockSpec((tm, hw), lambda s: (s, 5)),
            pl.BlockSpec((tm, d), lambda s: (s, 3)),
            pl.BlockSpec((tm, d), lambda s: (s, 4)),
            pl.BlockSpec((tm, hw), lambda s: (s, 0)),
            pl.BlockSpec((tm, hw), lambda s: (s, 0)),
            _const_spec((8, 6 * d)),
            _const_spec((1, d)),
            _const_spec((1, hw)),
            _const_spec((1, hw)),
            pl.BlockSpec((1,) + band.shape[1:], lambda s: (kind(s), 0, 0, 0)),
            pl.BlockSpec((1,) + inv.shape[1:], lambda s: (kind(s), 0, 0, 0)),
            _const_spec(w_pg.shape),
            _const_spec(w_bp.shape),
            _const_spec(w_bh.shape),
            _const_spec(w_out.shape),
            _const_spec(w_router_t.shape),
            _const_spec((n_e, 1)),
            _const_spec((tm, tm)),
        ],
        out_specs=[
            pl.BlockSpec((tm, d), lambda s: (s, 0)),
            pl.BlockSpec((tm, d // 2), lambda s: (s, 0)),
            pl.BlockSpec((1, 8, tm), lambda s: (s, 0, 0)),
            pl.BlockSpec((1, 8, tm), lambda s: (s, 0, 0)),
            pl.BlockSpec((n_e, 128), lambda s: (0, 0)),
        ],
        out_shape=[
            jax.ShapeDtypeStruct((ntot, d), F32),
            jax.ShapeDtypeStruct((ntot, d // 2), jnp.uint32),
            jax.ShapeDtypeStruct((n_tiles, 8, tm), jnp.int32),
            jax.ShapeDtypeStruct((n_tiles, 8, tm), F32),
            jax.ShapeDtypeStruct((n_e, 128), jnp.int32),
        ],
        scratch_shapes=[pltpu.VMEM((tm, hw), BF16), pltpu.VMEM((tm, hw), BF16), pltpu.VMEM((n_e, 128), F32)],
        compiler_params=_cparams(1),
    )(x, z, z, z, z, o_f, o_b, mods_l, norm2_w.reshape(1, d), pool_scale.reshape(1, hw),
      hg_norm_w.reshape(1, hw), band, inv, w_pg, w_bp, w_bh, w_out, w_router_t,
      router_bias.astype(F32).reshape(n_e, 1), before)


def _row_gather_start(src_hbm, idx_ref, dst, sem, n_rows):
    def body(it, carry):
        for k in range(2):
            r = 2 * it + k
            tok = idx_ref[0, 0, r]
            pltpu.make_async_copy(src_hbm.at[pl.ds(tok, 1)], dst.at[pl.ds(r, 1)], sem).start(priority=k)
        return carry

    lax.fori_loop(0, n_rows // 2, body, 0, unroll=4)


def _row_gather_wait(src_hbm, dst, sem, n_rows):
    pltpu.make_async_copy(src_hbm.at[pl.ds(0, n_rows)], dst, sem).wait()


def _dispatch_kernel(n_e, n_blk, tails_ref, pos_ref, h_ref, xs_hbm, zblk, sems):
    tm = h_ref.shape[0]
    rows = zblk.shape[0]
    sem = sems.at[0]
    zsem = sems.at[1]

    def for_each_pad(act):
        for e in range(n_e):
            def body(r, carry, e=e):
                act(pltpu.make_async_copy(zblk.at[pl.ds(0, 1)], xs_hbm.at[pl.ds(tails_ref[e] + r, 1)], zsem))
                return carry

            lax.fori_loop(0, tails_ref[n_e + e], body, 0)

        def blk_body(b, carry):
            act(pltpu.make_async_copy(zblk, xs_hbm.at[pl.ds(pl.multiple_of(b * rows, rows), rows)], zsem))
            return carry

        lax.fori_loop(tails_ref[2 * n_e], n_blk, blk_body, 0)

    @pl.when(pl.program_id(0) == 0)
    def _():
        zblk[...] = jnp.zeros_like(zblk)
        for_each_pad(lambda cp: cp.start())

    def body(r, carry):
        for k in range(2):
            dst = pos_ref[0, 0, k * tm + r]
            pltpu.make_async_copy(h_ref.at[pl.ds(r, 1)], xs_hbm.at[pl.ds(dst, 1)], sem).start(priority=k)
        return carry

    lax.fori_loop(0, tm, body, 0, unroll=8)
    pltpu.make_async_copy(xs_hbm.at[pl.ds(0, 2 * tm)], xs_hbm.at[pl.ds(0, 2 * tm)], sem).wait()

    @pl.when(pl.program_id(0) == 0)
    def _():
        for_each_pad(lambda cp: cp.wait())


def _dispatch(h2, pos, tails, n_blk):
    ntot, d = h2.shape
    tm = ROW_TILE
    n_tiles = ntot // tm
    n_e = (tails.shape[0] - 1) // 2
    grid_spec = pltpu.PrefetchScalarGridSpec(
        num_scalar_prefetch=1,
        grid=(n_tiles,),
        in_specs=[
            pl.BlockSpec((1, 1, 2 * tm), lambda i, t: (i, 0, 0), memory_space=pltpu.SMEM),
            pl.BlockSpec((tm, d), lambda i, t: (i, 0)),
        ],
        out_specs=pl.BlockSpec(memory_space=pl.ANY),
        scratch_shapes=[pltpu.VMEM((MOE_ROWS, d), h2.dtype), pltpu.SemaphoreType.DMA((2,))],
    )
    return pl.pallas_call(
        functools.partial(_dispatch_kernel, n_e, n_blk),
        name="dispatch",
        grid_spec=grid_spec,
        out_shape=jax.ShapeDtypeStruct((n_blk * MOE_ROWS, d), h2.dtype),
        compiler_params=_cparams(1),
    )(tails, pos, h2)


def _expert_kernel(blk_e_ref, nused_ref, x_ref, w1_ref, w3_ref, w2_ref, y_ref, w1c, w3c, w2c):
    i = pl.program_id(0)
    n_used = nused_ref[0]

    @pl.when(i < n_used)
    def _():
        changed = jnp.logical_or(i == 0, blk_e_ref[i] != blk_e_ref[jnp.maximum(i - 1, 0)])

        @pl.when(changed)
        def _():
            w1c[...] = w1_ref[0, 0].astype(BF16)
            w3c[...] = w3_ref[0, 0].astype(BF16)
            w2c[...] = w2_ref[0, 0].astype(BF16)

        xb = jnp.concatenate(_unpack_halves(x_ref[...]), axis=1).astype(BF16)
        h1 = _dot(xb, w1c[...])
        h3 = _dot(xb, w3c[...])
        act = (h1 * _sigmoid(h1) * h3).astype(BF16)
        y_ref[...] = _pack_halves(_dot(act, w2c[...]))

    @pl.when(i >= n_used)
    def _():
        y_ref[...] = jnp.zeros_like(y_ref)


def _experts(xs, blk_e, n_used, w1, w3, w2, layer):
    n_rows = xs.shape[0]
    _, n_e, d, f = w1.shape
    n_blk = blk_e.shape[0]
    rows = MOE_ROWS
    used = lambda i, nu: jnp.minimum(i, nu[0] - 1)
    grid_spec = pltpu.PrefetchScalarGridSpec(
        num_scalar_prefetch=2,
        grid=(n_blk,),
        in_specs=[
            pl.BlockSpec((rows, d // 2), lambda i, be, nu: (used(i, nu), 0)),
            pl.BlockSpec((1, 1, d, f), lambda i, be, nu: (layer, be[i], 0, 0)),
            pl.BlockSpec((1, 1, d, f), lambda i, be, nu: (layer, be[i], 0, 0)),
            pl.BlockSpec((1, 1, f, d), lambda i, be, nu: (layer, be[i], 0, 0)),
        ],
        out_specs=pl.BlockSpec((rows, d // 2), lambda i, be, nu: (i, 0)),
        scratch_shapes=[
            pltpu.VMEM((d, f), BF16),
            pltpu.VMEM((d, f), BF16),
            pltpu.VMEM((f, d), BF16),
        ],
    )
    return pl.pallas_call(
        _expert_kernel,
        name="experts",
        grid_spec=grid_spec,
        out_shape=jax.ShapeDtypeStruct((n_rows, d // 2), jnp.uint32),
        compiler_params=_cparams(1),
    )(blk_e, n_used, xs, w1, w3, w2)


def _combine_kernel(d, mode, tile_of, pos0_ref, posn_ref, y_hbm, x_ref, g_ref, mod_ref, *rest):
    if mode == "next":
        modn_ref, nw_ref, xo_ref, h_ref, rbuf, sem = rest
    else:
        fw_ref, out_ref, rbuf, sem = rest
    i = pl.program_id(0)
    n = pl.num_programs(0)
    tm = x_ref.shape[0]
    slot = i % 2

    @pl.when(i == 0)
    def _():
        _row_gather_start(y_hbm, pos0_ref, rbuf.at[0], sem.at[0], 2 * tm)

    @pl.when(i + 1 < n)
    def _():
        _row_gather_start(y_hbm, posn_ref, rbuf.at[1 - slot], sem.at[1 - slot], 2 * tm)

    _row_gather_wait(y_hbm, rbuf.at[slot], sem.at[slot], 2 * tm)
    mrow = tile_of(i)[1]
    m5 = mod_ref[pl.ds(mrow, 1), 5 * d:6 * d]
    g = g_ref[...]
    lo0, hi0 = _unpack_halves(rbuf[slot, 0:tm, :])
    lo1, hi1 = _unpack_halves(rbuf[slot, tm:2 * tm, :])
    g0, g1 = g[:, 0:1], g[:, 1:2]
    f = jnp.concatenate([g0 * lo0 + g1 * lo1, g0 * hi0 + g1 * hi1], axis=1)
    xn = x_ref[...] + m5 * f
    if mode == "next":
        xo_ref[...] = xn
        shift = modn_ref[pl.ds(mrow, 1), 0:d]
        scale = modn_ref[pl.ds(mrow, 1), d:2 * d]
        h_ref[...] = _norm_mod(xn, nw_ref[...], shift, scale).astype(BF16)
    else:
        out_ref[...] = xn * lax.rsqrt(jnp.mean(xn * xn, axis=-1, keepdims=True) + EPS) * fw_ref[...]


def _combine(x, yb, pos, gates, mods_l, n_batch, n_ctx, n_lat, mods_next=None, norm_next=None, final_w=None):
    ntot, d = x.shape
    tm = ROW_TILE
    tps, cps, lps = (n_ctx + n_lat) // tm, n_ctx // tm, n_lat // tm
    mode = "final" if final_w is not None else "next"
    if mode == "final":
        n_steps = n_batch * lps

        def tile_of(i):
            return (i // lps) * tps + cps + i % lps, i // lps
    else:
        n_steps = ntot // tm

        def tile_of(i):
            return i, _mod_row(i, tps, n_batch)

    tile = lambda i: tile_of(i)[0]
    in_specs = [
        pl.BlockSpec((1, 1, 2 * tm), lambda i: (tile(0), 0, 0), memory_space=pltpu.SMEM),
        pl.BlockSpec((1, 1, 2 * tm), lambda i: (tile(jnp.minimum(i + 1, n_steps - 1)), 0, 0),
                     memory_space=pltpu.SMEM),
        pl.BlockSpec(memory_space=pl.ANY),
        pl.BlockSpec((tm, d), lambda i: (tile(i), 0)),
        pl.BlockSpec((tm, 2), lambda i: (tile(i), 0)),
        _const_spec((8, 6 * d)),
    ]
    args = [pos, pos, yb, x, gates, mods_l]
    if mode == "next":
        in_specs += [pl.BlockSpec((8, 2 * d), lambda i: (0, 0)), pl.BlockSpec((1, d), lambda i: (0, 0))]
        args += [mods_next, norm_next.reshape(1, d)]
        out_specs = [pl.BlockSpec((tm, d), lambda i: (i, 0))] * 2
        out_shape = [jax.ShapeDtypeStruct((ntot, d), F32), jax.ShapeDtypeStruct((ntot, d), BF16)]
    else:
        in_specs += [pl.BlockSpec((1, d), lambda i: (0, 0))]
        args += [final_w.reshape(1, d)]
        out_specs = pl.BlockSpec((tm, d), lambda i: (i, 0))
        out_shape = jax.ShapeDtypeStruct((n_steps * tm, d), F32)
    return pl.pallas_call(
        functools.partial(_combine_kernel, d, mode, tile_of),
        name="combine",
        grid=(n_steps,),
        in_specs=in_specs,
        out_specs=out_specs,
        out_shape=out_shape,
        scratch_shapes=[pltpu.VMEM((2, 2 * tm, d // 2), jnp.uint32), pltpu.SemaphoreType.DMA((2,))],
        compiler_params=_cparams(1),
    )(*args)


def _dispatch_plan(eidx, counts, n_e):
    rows = MOE_ROWS
    n_tiles, _, tm = eidx.shape
    n_assign = n_tiles * tm * 2
    pcounts = (counts + rows - 1) // rows * rows
    pends = jnp.cumsum(pcounts)
    pstart = pends - pcounts
    experts = jnp.arange(n_e, dtype=jnp.int32)
    first = jnp.sum(jnp.where(eidx[:, 0:2, :, None] == experts, pstart, 0), axis=-1)
    dest = first + eidx[:, 2:4, :]
    n_blk = -(-n_assign // rows) + n_e
    blk_row0 = jnp.arange(n_blk, dtype=jnp.int32) * rows
    blk_e = jnp.minimum(jnp.sum(blk_row0[:, None] >= pends[None, :], axis=1), n_e - 1).astype(jnp.int32)
    n_used = (pends[-1] // rows).astype(jnp.int32).reshape(1)
    pos = dest.astype(jnp.int32).reshape(n_tiles, 1, 2 * tm)
    tails = jnp.concatenate([pstart + counts, pcounts - counts, n_used]).astype(jnp.int32)
    return blk_e, n_used, pos, n_blk, tails


def kernel(x, c, ctx, c_ctx, w_ada, b_ada, norm1_w, norm2_w, w_in, w_pool_group, pool_scale, hg_norm_w,
           lb_logits, w_branch_pool, w_branch_hgrn, w_out, w_router, router_bias, w_e1, w_e3, w_e2,
           final_norm_w):
    n_batch, n_lat, d = x.shape
    n_ctx = ctx.shape[1]
    depth = w_ada.shape[0]
    hw = hg_norm_w.shape[1]
    n_e = w_router.shape[1]
    t_sample = n_ctx + n_lat
    ntot = n_batch * t_sample
    ctx_row = n_batch
    assert n_batch < 8 and n_ctx == ROW_TILE and n_lat % ROW_TILE == 0

    cc = jnp.zeros((8, d), F32).at[:n_batch].set(c).at[ctx_row].set(c_ctx)
    mods = _ada_mods(cc, w_ada, b_ada)
    lb = _lower_bounds(lb_logits)
    w_router_t = jnp.transpose(w_router.astype(F32))

    xs, h = _norm_mod_rows(x, ctx, mods[0], norm1_w[0])
    for l in range(depth):
        z = _in_proj(h, w_in, l)
        o_f, o_b = _scan(z, lb, l, depth, n_batch, n_ctx, n_lat, hw)
        xs, h2, eidx, gate, counts = _merge(
            xs, z, o_f, o_b, mods[l], norm2_w[l], pool_scale[l], hg_norm_w[l], w_pool_group[l].astype(BF16),
            w_branch_pool[l].astype(BF16), w_branch_hgrn[l].astype(BF16), w_out[l].astype(BF16),
            w_router_t, router_bias, n_ctx, t_sample, ctx_row)
        blk_e, n_used, pos, n_blk, tails = _dispatch_plan(eidx, counts[:, 0], n_e)
        yb = _experts(_dispatch(h2, pos, tails, n_blk), blk_e, n_used, w_e1, w_e3, w_e2, l)
        g = jnp.transpose(gate[:, 0:2, :], (0, 2, 1)).reshape(ntot, 2)
        if l + 1 < depth:
            xs, h = _combine(xs, yb, pos, g, mods[l], n_batch, n_ctx, n_lat,
                             mods_next=mods[l + 1], norm_next=norm1_w[l + 1])
        else:
            out = _combine(xs, yb, pos, g, mods[l], n_batch, n_ctx, n_lat, final_w=final_norm_w)
    return out.reshape(n_batch, n_lat, d)
```

```python
import functools
import math

import numpy as np
import jax
import jax.numpy as jnp
from jax import lax
from jax.experimental import pallas as pl
from jax.experimental.pallas import tpu as pltpu

F32 = jnp.float32
BF16 = jnp.bfloat16

EPS = 1e-6
GRID_W = 64
POOL_WINDOWS = (2, 4, 8, 16)
HEAD_DIM = 128
N_EXPERT_GROUPS = 4
EXPERTS_PER_GROUP = 4
SCAN_CHUNK = 64
SCAN_GROUP = 4
ROW_TILE = 256
MOE_ROWS = 512
V7X_VMEM_BYTES = 64 * 1024 * 1024
VMEM_LIMIT = V7X_VMEM_BYTES - 8 * 1024 * 1024


def _cparams(n_axes, vmem=VMEM_LIMIT):
    return pltpu.CompilerParams(dimension_semantics=("arbitrary",) * n_axes, vmem_limit_bytes=vmem)


def _sigmoid(x):
    return 1.0 / (1.0 + jnp.exp(-x))


def _dot(a, b):
    return jnp.dot(a, b, preferred_element_type=F32)


def _dot_nt(a, b):
    return lax.dot_general(a, b, (((1,), (1,)), ((), ())), preferred_element_type=F32)


def _pack_halves(x):
    half = x.shape[1] // 2
    lo = lax.bitcast_convert_type(x[:, :half].astype(BF16).astype(F32), jnp.uint32)
    hi = lax.bitcast_convert_type(x[:, half:].astype(BF16).astype(F32), jnp.uint32)
    return (lo >> 16) | (hi & jnp.uint32(0xFFFF0000))


def _unpack_halves(w):
    lo = lax.bitcast_convert_type(w << 16, F32)
    hi = lax.bitcast_convert_type(w & jnp.uint32(0xFFFF0000), F32)
    return lo, hi


def _lb_kernel(depth, logit_ref, lb_ref):
    for d in range(2):
        rows = [logit_ref[d * depth + i:d * depth + i + 1, :] for i in range(depth)]
        m = rows[0]
        for r in rows[1:]:
            m = jnp.maximum(m, r)
        es = [jnp.exp(r - m) for r in rows]
        tot = es[0]
        for e in es[1:]:
            tot = tot + e
        ws = [e / tot for e in es]
        acc = jnp.zeros_like(ws[0])
        for i in range(depth):
            acc = acc + ws[i]
            lb_ref[d * depth + i:d * depth + i + 1, :] = acc - ws[0]


def _lower_bounds(lb_logits):
    _, depth, hw = lb_logits.shape
    flat = lb_logits.astype(F32).reshape(2 * depth, hw)
    out = pl.pallas_call(
        functools.partial(_lb_kernel, depth),
        name="lower_bounds",
        out_shape=jax.ShapeDtypeStruct((2 * depth, hw), F32),
    )(flat)
    return out.reshape(2 * depth, 1, hw)


def _ada_kernel(c_ref, w_ref, b_ref, o_ref):
    c = c_ref[...]
    s = (c * _sigmoid(c)).astype(BF16)
    o_ref[0] = _dot(s, w_ref[0].astype(BF16)) + b_ref[0]


def _ada_mods(cc, w_ada, b_ada):
    depth, d, n6 = w_ada.shape
    tn = 1024
    return pl.pallas_call(
        _ada_kernel,
        name="ada_mods",
        grid=(depth, n6 // tn),
        in_specs=[
            pl.BlockSpec((8, d), lambda l, j: (0, 0)),
            pl.BlockSpec((1, d, tn), lambda l, j: (l, 0, j)),
            pl.BlockSpec((1, 1, tn), lambda l, j: (l, 0, j)),
        ],
        out_specs=pl.BlockSpec((1, 8, tn), lambda l, j: (l, 0, j)),
        out_shape=jax.ShapeDtypeStruct((depth, 8, n6), F32),
        compiler_params=_cparams(2),
    )(cc, w_ada, b_ada.reshape(depth, 1, n6))


INPROJ_MAX_ROWS = 2176
INPROJ_COLS = 1024


def _mod_row(i, tiles_per_sample, ctx_row):
    return jnp.where(i % tiles_per_sample == 0, ctx_row, i // tiles_per_sample)


def _norm_mod(x, nw, shift, scale):
    n = x * lax.rsqrt(jnp.mean(x * x, axis=-1, keepdims=True) + EPS) * nw
    return n * (1.0 + scale) + shift


def _normmod_kernel(d, tiles_per_sample, ctx_row, x_ref, c_ref, mod_ref, nw_ref, xs_ref, h_ref):
    i = pl.program_id(0)
    mrow = _mod_row(i, tiles_per_sample, ctx_row)
    shift = mod_ref[pl.ds(mrow, 1), 0:d]
    scale = mod_ref[pl.ds(mrow, 1), d:2 * d]

    def emit(src_ref):
        x = src_ref[...]
        xs_ref[...] = x
        h_ref[...] = _norm_mod(x, nw_ref[...], shift, scale).astype(BF16)

    is_ctx = i % tiles_per_sample == 0
    pl.when(is_ctx)(lambda: emit(c_ref))
    pl.when(jnp.logical_not(is_ctx))(lambda: emit(x_ref))


def _norm_mod_rows(x, ctx, mods_l, norm_w):
    n_batch, n_lat, d = x.shape
    n_ctx = ctx.shape[1]
    tm = ROW_TILE
    assert n_ctx == tm
    lps = n_lat // tm
    tps = lps + 1
    ntot = n_batch * (n_ctx + n_lat)
    return pl.pallas_call(
        functools.partial(_normmod_kernel, d, tps, n_batch),
        name="norm_mod",
        grid=(n_batch * tps,),
        in_specs=[
            pl.BlockSpec((tm, d), lambda i: ((i // tps) * lps + jnp.maximum(i % tps - 1, 0), 0)),
            pl.BlockSpec((tm, d), lambda i: (i // tps, 0)),
            pl.BlockSpec((8, 2 * d), lambda i: (0, 0)),
            pl.BlockSpec((1, d), lambda i: (0, 0)),
        ],
        out_specs=[pl.BlockSpec((tm, d), lambda i: (i, 0))] * 2,
        out_shape=[jax.ShapeDtypeStruct((ntot, d), F32), jax.ShapeDtypeStruct((ntot, d), BF16)],
        compiler_params=_cparams(1),
    )(x.reshape(n_batch * n_lat, d), ctx.reshape(n_batch * n_ctx, d), mods_l, norm_w.reshape(1, d))


def _inproj_kernel(h_ref, w_ref, o_ref):
    o_ref[...] = _dot(h_ref[...], w_ref[0].astype(BF16)).astype(BF16)


def _in_proj(h, w_in, layer):
    ntot, d = h.shape
    n_cols = w_in.shape[2]
    tn = INPROJ_COLS
    tm = max(k for k in range(16, INPROJ_MAX_ROWS + 1, 16) if ntot % k == 0)
    return pl.pallas_call(
        _inproj_kernel,
        name="in_proj",
        grid=(ntot // tm, n_cols // tn),
        in_specs=[
            pl.BlockSpec((tm, d), lambda r, j: (r, 0)),
            pl.BlockSpec((1, d, tn), lambda r, j: (layer, 0, j)),
        ],
        out_specs=pl.BlockSpec((tm, tn), lambda r, j: (r, j)),
        out_shape=jax.ShapeDtypeStruct((ntot, n_cols), BF16),
        compiler_params=_cparams(2),
    )(h, w_in)


COMPACT_MIN_HALF = 8
MXU_LEVEL_MIN_HALF = 2


def _scan_levels(c):
    out, m = [], c // 2
    while m >= MXU_LEVEL_MIN_HALF:
        out.append(m)
        m //= 2
    return out


def _scan_row_layout(c):
    off = {"b": 0, "kend": c, "tot": 2 * c}
    pos = 2 * c + 8
    for m in _scan_levels(c):
        n = c // 2 if m >= COMPACT_MIN_HALF else c
        off[("q", m)] = pos
        off[("k", m)] = pos + n
        pos += 2 * n
    return off, pos


def _scan_constants(c):
    t = np.arange(c)[:, None]
    r = np.arange(c)[None, :]
    out = []
    for flip in (False, True):
        fl = (lambda a: a[::-1, ::-1]) if flip else (lambda a: a)
        sets = [fl(r <= t), fl(r > t), np.ones((8, c), bool)]
        for m in _scan_levels(c):
            mid = (t // (2 * m)) * (2 * m) + m
            qa = fl((t >= mid) & (r >= mid) & (r <= t))
            ka = fl((t < mid) & (r > t) & (r <= mid - 1))
            if m >= COMPACT_MIN_HALF:
                rows = np.arange(c)
                q_valid = (rows % (2 * m) >= m) != flip
                sets += [qa[q_valid], ka[~q_valid]]
            else:
                sets += [qa, ka]
        mat = np.concatenate(sets, axis=0).astype(np.float32)
        out.append(np.concatenate([mat, mat], axis=1))
    return np.stack(out)


def _tile_pairs(e, m, c):
    parts = []
    for p in range(c // (2 * m)):
        blk = e[p * m:(p + 1) * m]
        parts += [blk, blk]
    return jnp.concatenate(parts, axis=0)


def _scan_gates(c, q_ref, f_ref, lb_ref, mat, e_ref):
    zq = q_ref[...].astype(F32)
    zf = f_ref[...].astype(F32)
    lbv = lb_ref[0]
    f = lbv + (1.0 - lbv) * _sigmoid(zf)
    kk = 1.0 - f
    qq = zq * _sigmoid(zq)
    lf2 = jnp.log(f) * math.log2(math.e)
    lf_hi = lf2.astype(BF16)
    lf_lo = (lf2 - lf_hi.astype(F32)).astype(BF16)
    e_ref[...] = jnp.exp2(_dot(mat, jnp.concatenate([lf_hi, lf_lo], axis=0)))
    row = lax.broadcasted_iota(jnp.int32, (c, 1), 0)
    k_sw = jnp.where((row & 1) == 1, pltpu.roll(kk, 1, axis=0), pltpu.roll(kk, c - 1, axis=0))
    return qq, kk, qq * f, k_sw


def _scan_intra(n_heads, c, backward, qq, kk, qf, k_sw, e_ref, a_ref):
    levels = _scan_levels(c)
    off, _ = _scan_row_layout(c)
    ti = lax.broadcasted_iota(jnp.int32, (c, c), 0)
    si = lax.broadcasted_iota(jnp.int32, (c, c), 1)
    same = {m: (ti & -(2 * m)) == (si & -(2 * m)) for m in levels[1:] + [1]}
    eye = ti == si
    causal = (ti <= si) if backward else (ti >= si)
    for h in range(n_heads):
        cs = slice(h * HEAD_DIM, (h + 1) * HEAD_DIM)
        qh = qq[:, cs]
        kh = kk[:, cs]
        a = None
        for m in levels:
            n = c // 2 if m >= COMPACT_MIN_HALF else c
            eq = e_ref[off[("q", m)]:off[("q", m)] + n, cs]
            ek = e_ref[off[("k", m)]:off[("k", m)] + n, cs]
            if m >= COMPACT_MIN_HALF:
                eq, ek = _tile_pairs(eq, m, c), _tile_pairs(ek, m, c)
            p = _dot_nt((qh * eq).astype(BF16), (kh * ek).astype(BF16))
            a = p if a is None else jnp.where(same[m], p, a)
        pair = jnp.sum(qf[:, cs] * k_sw[:, cs], axis=-1, keepdims=True)
        diag = jnp.sum(qh * kh, axis=-1, keepdims=True)
        a = jnp.where(same[1], pair, a)
        a = jnp.where(eye, diag, a)
        a_ref[h] = jnp.where(causal, a, 0.0).astype(BF16)


def _scan_state(n_heads, c, qq, kk, i_ref, e_ref, a_ref, s_ref, o_ref):
    off, _ = _scan_row_layout(c)
    for h in range(n_heads):
        cs = slice(h * HEAD_DIM, (h + 1) * HEAD_DIM)
        qh = qq[:, cs]
        kh = kk[:, cs]
        vh = i_ref[:, cs]
        st = s_ref[h]
        qb = (qh * e_ref[off["b"]:off["b"] + c, cs]).astype(BF16)
        o = _dot_nt(qb, st.astype(BF16)) + _dot(a_ref[h], vh)
        o_ref[:, cs] = o.astype(BF16)
        vt = vh.astype(F32).T.astype(BF16)
        kt = (kh * e_ref[off["kend"]:off["kend"] + c, cs]).astype(BF16)
        s_ref[h] = st * e_ref[off["tot"]:off["tot"] + 1, cs] + _dot(vt, kt)


def _scan_kernel(n_heads, c, qf_ref, qb_ref, ff_ref, fb_ref, if_ref, ib_ref, lbf_ref, lbb_ref, m_ref,
                 of_ref, ob_ref, s_ref, e_ref, a_ref):
    @pl.when(pl.program_id(1) == 0)
    def _():
        s_ref[...] = jnp.zeros_like(s_ref)

    dirs = ((qf_ref, ff_ref, if_ref, lbf_ref, of_ref), (qb_ref, fb_ref, ib_ref, lbb_ref, ob_ref))
    work = [(d, pl.ds((SCAN_GROUP - 1 - k if d else k) * c, c)) for d in range(2) for k in range(SCAN_GROUP)]
    gates = []
    for n, (d, rows) in enumerate(work):
        q, f, _, lb, _ = dirs[d]
        gates.append(_scan_gates(c, q.at[rows], f.at[rows], lb, m_ref[d], e_ref.at[n]))
    for n, (d, rows) in enumerate(work):
        _scan_intra(n_heads, c, d == 1, *gates[n], e_ref.at[n], a_ref.at[n])
    for n, (d, rows) in enumerate(work):
        _, _, i_ref, _, o_ref = dirs[d]
        _scan_state(n_heads, c, gates[n][0], gates[n][1], i_ref.at[rows], e_ref.at[n], a_ref.at[n],
                    s_ref.at[d], o_ref.at[rows])


def _scan(z, lb, layer, depth, n_batch, n_ctx, n_lat, hw):
    c = SCAN_CHUNK
    blk = SCAN_GROUP * c
    n_heads = hw // HEAD_DIM
    assert n_ctx % blk == 0 and n_lat % blk == 0
    nc, nl = n_ctx // blk, n_lat // blk
    steps = nc + nl
    mats = jnp.asarray(_scan_constants(c), BF16)
    n_rows = mats.shape[1]
    assert n_rows == _scan_row_layout(c)[1]
    ntot = z.shape[0]

    def fwd(b, s):
        return b * steps + s

    def bwd(b, s):
        return b * steps + jnp.where(s < nc, nc - 1 - s, nc + (steps - 1 - s))

    def col(chunk, j):
        return pl.BlockSpec((blk, hw), lambda b, s: (chunk(b, s), j))

    lb_spec = lambda d: pl.BlockSpec((1, 1, hw), lambda b, s: (d * depth + layer, 0, 0))
    return pl.pallas_call(
        functools.partial(_scan_kernel, n_heads, c),
        name="hgrn2_scan",
        grid=(n_batch, steps),
        in_specs=[
            col(fwd, 1), col(bwd, 1),
            col(fwd, 2), col(bwd, 3),
            col(fwd, 4), col(bwd, 4),
            lb_spec(0), lb_spec(1),
            pl.BlockSpec((2, n_rows, 2 * c), lambda b, s: (0, 0, 0)),
        ],
        out_specs=[pl.BlockSpec((blk, hw), lambda b, s: (fwd(b, s), 0)),
                   pl.BlockSpec((blk, hw), lambda b, s: (bwd(b, s), 0))],
        out_shape=[jax.ShapeDtypeStruct((ntot, hw), BF16)] * 2,
        scratch_shapes=[pltpu.VMEM((2, n_heads, HEAD_DIM, HEAD_DIM), F32),
                        pltpu.VMEM((2 * SCAN_GROUP, n_rows, hw), F32),
                        pltpu.VMEM((2 * SCAN_GROUP, n_heads, c, c), BF16)],
        compiler_params=_cparams(2),
    )(z, z, z, z, z, z, lb, lb, mats)


def _pool_constants(tile, n_ctx):
    assert n_ctx == tile and tile % GRID_W == 0
    band = np.zeros((2, len(POOL_WINDOWS), tile, tile), np.float32)
    inv = np.zeros((2, len(POOL_WINDOWS), tile, 1), np.float32)
    for kind, row_len in enumerate((n_ctx, GRID_W)):
        t = np.arange(tile)
        tau = t % row_len
        base = t - tau
        for g, w in enumerate(POOL_WINDOWS):
            lo = np.maximum(tau - w // 2, 0)
            hi = np.minimum(tau + w // 2 - 1, row_len - 1)
            s = np.arange(tile)[None, :]
            band[kind, g] = (s >= (base + lo)[:, None]) & (s <= (base + hi)[:, None])
            inv[kind, g, :, 0] = 1.0 / (hi - lo + 1)
    return band, inv


def _route(sel, sc):
    n_g, per = N_EXPERT_GROUPS, EXPERTS_PER_GROUP
    row = lambda a, e: a[e:e + 1, :]
    best = None
    for g in range(n_g):
        rows = [row(sel, g * per + i) for i in range(per)]
        gs = None
        for i in range(per):
            for j in range(i + 1, per):
                p = rows[i] + rows[j]
                gs = p if gs is None else jnp.maximum(gs, p)
        if best is None:
            best, g_idx = gs, jnp.zeros(gs.shape, jnp.int32)
        else:
            upd = gs > best
            g_idx = jnp.where(upd, g, g_idx)
            best = jnp.where(upd, gs, best)
    cand, raw = [], []
    for i in range(per):
        ci, ri = row(sel, i), row(sc, i)
        for g in range(1, n_g):
            ci = jnp.where(g_idx == g, row(sel, g * per + i), ci)
            ri = jnp.where(g_idx == g, row(sc, g * per + i), ri)
        cand.append(ci)
        raw.append(ri)

    def argmax4(vals):
        m, loc = vals[0], jnp.zeros(vals[0].shape, jnp.int32)
        for i in range(1, per):
            upd = vals[i] > m
            loc = jnp.where(upd, i, loc)
            m = jnp.where(upd, vals[i], m)
        return loc

    loc0 = argmax4(cand)
    loc1 = argmax4([jnp.where(loc0 == i, -jnp.inf, cand[i]) for i in range(per)])

    def pick(loc):
        out = raw[0]
        for i in range(1, per):
            out = jnp.where(loc == i, raw[i], out)
        return out

    g0, g1 = pick(loc0), pick(loc1)
    tot = g0 + g1
    return g_idx * per + loc0, g_idx * per + loc1, g0 / tot, g1 / tot


def _merge_kernel(d, hw, tiles_per_sample, ctx_row,
                  x_ref, p_ref, og_ref, gp_ref, gh_ref, of_ref, ob_ref, mod_ref, n2_ref, ps_ref, hn_ref,
                  band_ref, inv_ref, wpg_ref, wbp_ref, wbh_ref, wout_ref, wr_ref, rb_ref, tri_ref,
                  xo_ref, h2_ref, eidx_ref, gate_ref, cnt_ref, pool_scr, hg_scr, carry_ref):
    i = pl.program_id(0)
    n_heads = hw // HEAD_DIM
    n_groups = len(POOL_WINDOWS)
    pg = p_ref.shape[1] // n_groups
    mrow = _mod_row(i, tiles_per_sample, ctx_row)
    mod = lambda k: mod_ref[pl.ds(mrow, 1), k * d:(k + 1) * d]

    for h in range(n_heads):
        cs = slice(h * HEAD_DIM, (h + 1) * HEAD_DIM)
        o = of_ref[:, cs].astype(F32) + ob_ref[:, cs].astype(F32)
        o = o * lax.rsqrt(jnp.mean(o * o, axis=-1, keepdims=True) + EPS)
        og = og_ref[:, cs].astype(F32)
        hg_scr[:, cs] = (o * hn_ref[:, cs] * (og * _sigmoid(og))).astype(BF16)

    for g in range(n_groups):
        cs = slice(g * pg, (g + 1) * pg)
        u = p_ref[:, cs]
        wsum = _dot(band_ref[0, g], u)
        dlt = wsum * inv_ref[0, g] - u.astype(F32)
        pool_scr[:, cs] = (_dot(dlt.astype(BF16), wpg_ref[g]) * ps_ref[:, cs]).astype(BF16)

    gp = gp_ref[...].astype(F32)
    gh = gh_ref[...].astype(F32)
    m = _sigmoid(gp) * _dot(pool_scr[...], wbp_ref[...]) + _sigmoid(gh) * _dot(hg_scr[...], wbh_ref[...])
    y = _dot(m.astype(BF16), wout_ref[...])
    xn = x_ref[...] + mod(2) * y
    xo_ref[...] = xn
    n2 = xn * lax.rsqrt(jnp.mean(xn * xn, axis=-1, keepdims=True) + EPS) * n2_ref[...]
    h2 = n2 * (1.0 + mod(4)) + mod(3)
    h2_ref[...] = _pack_halves(h2)

    sc = _sigmoid(_dot_nt(wr_ref[...].astype(BF16), h2.astype(BF16)))
    e0, e1, g0, g1 = _route(sc + rb_ref[...], sc)

    @pl.when(i == 0)
    def _():
        carry_ref[...] = jnp.zeros_like(carry_ref)

    er = lax.broadcasted_iota(jnp.int32, sc.shape, 0)
    oh0 = (er == e0).astype(F32)
    oh1 = (er == e1).astype(F32)
    both = oh0 + oh1
    base = carry_ref[:, 0:1] + _dot(both.astype(BF16), tri_ref[...])
    r0 = jnp.sum(oh0 * base, axis=0, keepdims=True)
    r1 = jnp.sum(oh1 * base, axis=0, keepdims=True)
    carry_ref[...] = carry_ref[...] + jnp.sum(both, axis=1, keepdims=True)
    cnt_ref[...] = carry_ref[...].astype(jnp.int32)

    eidx_ref[...] = jnp.zeros_like(eidx_ref)
    gate_ref[...] = jnp.zeros_like(gate_ref)
    eidx_ref[0, 0:1, :] = e0
    eidx_ref[0, 1:2, :] = e1
    eidx_ref[0, 2:3, :] = r0.astype(jnp.int32)
    eidx_ref[0, 3:4, :] = r1.astype(jnp.int32)
    gate_ref[0, 0:1, :] = g0
    gate_ref[0, 1:2, :] = g1


def _const_spec(shape):
    nd = len(shape)
    return pl.BlockSpec(shape, lambda i: (0,) * nd, pipeline_mode=pl.Buffered(1))


def _merge(x, z, o_f, o_b, mods_l, norm2_w, pool_scale, hg_norm_w, w_pg, w_bp, w_bh, w_out, w_router_t,
           router_bias, n_ctx, t_sample, ctx_row):
    ntot, d = x.shape
    hw = o_f.shape[1]
    d_pool = w_bp.shape[0]
    assert d_pool == hw and d == 2 * hw
    tm = ROW_TILE
    tps = t_sample // tm
    n_e = w_router_t.shape[0]
    band, inv = _pool_constants(tm, n_ctx)
    band = jnp.asarray(band, BF16)
    inv = jnp.asarray(inv, F32)
    kind = lambda i: jnp.where(i % tps == 0, 0, 1)
    n_tiles = ntot // tm
    before = jnp.asarray(np.triu(np.ones((tm, tm), np.float32), 1), BF16)
    return pl.pallas_call(
        functools.partial(_merge_kernel, d, hw, tps, ctx_row),
        name="merge",
        grid=(n_tiles,),
        in_specs=[
            pl.BlockSpec((tm, d), lambda s: (s, 0)),
            pl.BlockSpec((tm, hw), lambda s: (s, 0)),
            pl.BlockSpec((tm, hw), lambda s: (s, 5)),
            pl.BlockSpec((tm, d), lambda s: (s, 3)),
            pl.BlockSpec((tm, d), lambda s: (s, 4)),
            pl.BlockSpec((tm, hw), lambda s: (s, 0)),
            pl.BlockSpec((tm, hw), lambda s: (s, 0)),
            _const_spec((8, 6 * d)),
            _const_spec((1, d)),
            _const_spec((1, hw)),
            _const_spec((1, hw)),
            pl.BlockSpec((1,) + band.shape[1:], lambda s: (kind(s), 0, 0, 0)),
            pl.BlockSpec((1,) + inv.shape[1:], lambda s: (kind(s), 0, 0, 0)),
            _const_spec(w_pg.shape),
            _const_spec(w_bp.shape),
            _const_spec(w_bh.shape),
            _const_spec(w_out.shape),
            _const_spec(w_router_t.shape),
            _const_spec((n_e, 1)),
            _const_spec((tm, tm)),
        ],
        out_specs=[
            pl.BlockSpec((tm, d), lambda s: (s, 0)),
            pl.BlockSpec((tm, d // 2), lambda s: (s, 0)),
            pl.BlockSpec((1, 8, tm), lambda s: (s, 0, 0)),
            pl.BlockSpec((1, 8, tm), lambda s: (s, 0, 0)),
            pl.BlockSpec((n_e, 128), lambda s: (0, 0)),
        ],
        out_shape=[
            jax.ShapeDtypeStruct((ntot, d), F32),
            jax.ShapeDtypeStruct((ntot, d // 2), jnp.uint32),
            jax.ShapeDtypeStruct((n_tiles, 8, tm), jnp.int32),
            jax.ShapeDtypeStruct((n_tiles, 8, tm), F32),
            jax.ShapeDtypeStruct((n_e, 128), jnp.int32),
        ],
        scratch_shapes=[pltpu.VMEM((tm, hw), BF16), pltpu.VMEM((tm, hw), BF16), pltpu.VMEM((n_e, 128), F32)],
        compiler_params=_cparams(1),
    )(x, z, z, z, z, o_f, o_b, mods_l, norm2_w.reshape(1, d), pool_scale.reshape(1, hw),
      hg_norm_w.reshape(1, hw), band, inv, w_pg, w_bp, w_bh, w_out, w_router_t,
      router_bias.astype(F32).reshape(n_e, 1), before)


def _row_gather_start(src_hbm, idx_ref, dst, sem, n_rows):
    def body(it, carry):
        for k in range(2):
            r = 2 * it + k
            tok = idx_ref[0, 0, r]
            pltpu.make_async_copy(src_hbm.at[pl.ds(tok, 1)], dst.at[pl.ds(r, 1)], sem).start(priority=k)
        return carry

    lax.fori_loop(0, n_rows // 2, body, 0, unroll=4)


def _row_gather_wait(src_hbm, dst, sem, n_rows):
    pltpu.make_async_copy(src_hbm.at[pl.ds(0, n_rows)], dst, sem).wait()


def _dispatch_kernel(n_e, n_blk, tails_ref, pos_ref, h_ref, xs_hbm, zblk, sems):
    tm = h_ref.shape[0]
    rows = zblk.shape[0]
    sem = sems.at[0]
    zsem = sems.at[1]

    def for_each_pad(act):
        for e in range(n_e):
            def body(r, carry, e=e):
                act(pltpu.make_async_copy(zblk.at[pl.ds(0, 1)], xs_hbm.at[pl.ds(tails_ref[e] + r, 1)], zsem))
                return carry

            lax.fori_loop(0, tails_ref[n_e + e], body, 0)

        def blk_body(b, carry):
            act(pltpu.make_async_copy(zblk, xs_hbm.at[pl.ds(pl.multiple_of(b * rows, rows), rows)], zsem))
            return carry

        lax.fori_loop(tails_ref[2 * n_e], n_blk, blk_body, 0)

    @pl.when(pl.program_id(0) == 0)
    def _():
        zblk[...] = jnp.zeros_like(zblk)
        for_each_pad(lambda cp: cp.start())

    def body(r, carry):
        for k in range(2):
            dst = pos_ref[0, 0, k * tm + r]
            pltpu.make_async_copy(h_ref.at[pl.ds(r, 1)], xs_hbm.at[pl.ds(dst, 1)], sem).start(priority=k)
        return carry

    lax.fori_loop(0, tm, body, 0, unroll=8)
    pltpu.make_async_copy(xs_hbm.at[pl.ds(0, 2 * tm)], xs_hbm.at[pl.ds(0, 2 * tm)], sem).wait()

    @pl.when(pl.program_id(0) == 0)
    def _():
        for_each_pad(lambda cp: cp.wait())


def _dispatch(h2, pos, tails, n_blk):
    ntot, d = h2.shape
    tm = ROW_TILE
    n_tiles = ntot // tm
    n_e = (tails.shape[0] - 1) // 2
    grid_spec = pltpu.PrefetchScalarGridSpec(
        num_scalar_prefetch=1,
        grid=(n_tiles,),
        in_specs=[
            pl.BlockSpec((1, 1, 2 * tm), lambda i, t: (i, 0, 0), memory_space=pltpu.SMEM),
            pl.BlockSpec((tm, d), lambda i, t: (i, 0)),
        ],
        out_specs=pl.BlockSpec(memory_space=pl.ANY),
        scratch_shapes=[pltpu.VMEM((MOE_ROWS, d), h2.dtype), pltpu.SemaphoreType.DMA((2,))],
    )
    return pl.pallas_call(
        functools.partial(_dispatch_kernel, n_e, n_blk),
        name="dispatch",
        grid_spec=grid_spec,
        out_shape=jax.ShapeDtypeStruct((n_blk * MOE_ROWS, d), h2.dtype),
        compiler_params=_cparams(1),
    )(tails, pos, h2)


def _expert_kernel(blk_e_ref, nused_ref, x_ref, w1_ref, w3_ref, w2_ref, y_ref, w1c, w3c, w2c):
    i = pl.program_id(0)
    n_used = nused_ref[0]

    @pl.when(i < n_used)
    def _():
        changed = jnp.logical_or(i == 0, blk_e_ref[i] != blk_e_ref[jnp.maximum(i - 1, 0)])

        @pl.when(changed)
        def _():
            w1c[...] = w1_ref[0, 0].astype(BF16)
            w3c[...] = w3_ref[0, 0].astype(BF16)
            w2c[...] = w2_ref[0, 0].astype(BF16)

        xb = jnp.concatenate(_unpack_halves(x_ref[...]), axis=1).astype(BF16)
        h1 = _dot(xb, w1c[...])
        h3 = _dot(xb, w3c[...])
        act = (h1 * _sigmoid(h1) * h3).astype(BF16)
        y_ref[...] = _pack_halves(_dot(act, w2c[...]))

    @pl.when(i >= n_used)
    def _():
        y_ref[...] = jnp.zeros_like(y_ref)


def _experts(xs, blk_e, n_used, w1, w3, w2, layer):
    n_rows = xs.shape[0]
    _, n_e, d, f = w1.shape
    n_blk = blk_e.shape[0]
    rows = MOE_ROWS
    used = lambda i, nu: jnp.minimum(i, nu[0] - 1)
    grid_spec = pltpu.PrefetchScalarGridSpec(
        num_scalar_prefetch=2,
        grid=(n_blk,),
        in_specs=[
            pl.BlockSpec((rows, d // 2), lambda i, be, nu: (used(i, nu), 0)),
            pl.BlockSpec((1, 1, d, f), lambda i, be, nu: (layer, be[i], 0, 0)),
            pl.BlockSpec((1, 1, d, f), lambda i, be, nu: (layer, be[i], 0, 0)),
            pl.BlockSpec((1, 1, f, d), lambda i, be, nu: (layer, be[i], 0, 0)),
        ],
        out_specs=pl.BlockSpec((rows, d // 2), lambda i, be, nu: (i, 0)),
        scratch_shapes=[
            pltpu.VMEM((d, f), BF16),
            pltpu.VMEM((d, f), BF16),
            pltpu.VMEM((f, d), BF16),
        ],
    )
    return pl.pallas_call(
        _expert_kernel,
        name="experts",
        grid_spec=grid_spec,
        out_shape=jax.ShapeDtypeStruct((n_rows, d // 2), jnp.uint32),
        compiler_params=_cparams(1),
    )(blk_e, n_used, xs, w1, w3, w2)


def _combine_kernel(d, mode, tile_of, pos0_ref, posn_ref, y_hbm, x_ref, g_ref, mod_ref, *rest):
    if mode == "next":
        modn_ref, nw_ref, xo_ref, h_ref, rbuf, sem = rest
    else:
        fw_ref, out_ref, rbuf, sem = rest
    i = pl.program_id(0)
    n = pl.num_programs(0)
    tm = x_ref.shape[0]
    slot = i % 2

    @pl.when(i == 0)
    def _():
        _row_gather_start(y_hbm, pos0_ref, rbuf.at[0], sem.at[0], 2 * tm)

    @pl.when(i + 1 < n)
    def _():
        _row_gather_start(y_hbm, posn_ref, rbuf.at[1 - slot], sem.at[1 - slot], 2 * tm)

    _row_gather_wait(y_hbm, rbuf.at[slot], sem.at[slot], 2 * tm)
    mrow = tile_of(i)[1]
    m5 = mod_ref[pl.ds(mrow, 1), 5 * d:6 * d]
    g = g_ref[...]
    lo0, hi0 = _unpack_halves(rbuf[slot, 0:tm, :])
    lo1, hi1 = _unpack_halves(rbuf[slot, tm:2 * tm, :])
    g0, g1 = g[:, 0:1], g[:, 1:2]
    f = jnp.concatenate([g0 * lo0 + g1 * lo1, g0 * hi0 + g1 * hi1], axis=1)
    xn = x_ref[...] + m5 * f
    if mode == "next":
        xo_ref[...] = xn
        shift = modn_ref[pl.ds(mrow, 1), 0:d]
        scale = modn_ref[pl.ds(mrow, 1), d:2 * d]
        h_ref[...] = _norm_mod(xn, nw_ref[...], shift, scale).astype(BF16)
    else:
        out_ref[...] = xn * lax.rsqrt(jnp.mean(xn * xn, axis=-1, keepdims=True) + EPS) * fw_ref[...]


def _combine(x, yb, pos, gates, mods_l, n_batch, n_ctx, n_lat, mods_next=None, norm_next=None, final_w=None):
    ntot, d = x.shape
    tm = ROW_TILE
    tps, cps, lps = (n_ctx + n_lat) // tm, n_ctx // tm, n_lat // tm
    mode = "final" if final_w is not None else "next"
    if mode == "final":
        n_steps = n_batch * lps

        def tile_of(i):
            return (i // lps) * tps + cps + i % lps, i // lps
    else:
        n_steps = ntot // tm

        def tile_of(i):
            return i, _mod_row(i, tps, n_batch)

    tile = lambda i: tile_of(i)[0]
    in_specs = [
        pl.BlockSpec((1, 1, 2 * tm), lambda i: (tile(0), 0, 0), memory_space=pltpu.SMEM),
        pl.BlockSpec((1, 1, 2 * tm), lambda i: (tile(jnp.minimum(i + 1, n_steps - 1)), 0, 0),
                     memory_space=pltpu.SMEM),
        pl.BlockSpec(memory_space=pl.ANY),
        pl.BlockSpec((tm, d), lambda i: (tile(i), 0)),
        pl.BlockSpec((tm, 2), lambda i: (tile(i), 0)),
        _const_spec((8, 6 * d)),
    ]
    args = [pos, pos, yb, x, gates, mods_l]
    if mode == "next":
        in_specs += [pl.BlockSpec((8, 2 * d), lambda i: (0, 0)), pl.BlockSpec((1, d), lambda i: (0, 0))]
        args += [mods_next, norm_next.reshape(1, d)]
        out_specs = [pl.BlockSpec((tm, d), lambda i: (i, 0))] * 2
        out_shape = [jax.ShapeDtypeStruct((ntot, d), F32), jax.ShapeDtypeStruct((ntot, d), BF16)]
    else:
        in_specs += [pl.BlockSpec((1, d), lambda i: (0, 0))]
        args += [final_w.reshape(1, d)]
        out_specs = pl.BlockSpec((tm, d), lambda i: (i, 0))
        out_shape = jax.ShapeDtypeStruct((n_steps * tm, d), F32)
    return pl.pallas_call(
        functools.partial(_combine_kernel, d, mode, tile_of),
        name="combine",
        grid=(n_steps,),
        in_specs=in_specs,
        out_specs=out_specs,
        out_shape=out_shape,
        scratch_shapes=[pltpu.VMEM((2, 2 * tm, d // 2), jnp.uint32), pltpu.SemaphoreType.DMA((2,))],
        compiler_params=_cparams(1),
    )(*args)


def _dispatch_plan(eidx, counts, n_e):
    rows = MOE_ROWS
    n_tiles, _, tm = eidx.shape
    n_assign = n_tiles * tm * 2
    pcounts = (counts + rows - 1) // rows * rows
    pends = jnp.cumsum(pcounts)
    pstart = pends - pcounts
    experts = jnp.arange(n_e, dtype=jnp.int32)
    first = jnp.sum(jnp.where(eidx[:, 0:2, :, None] == experts, pstart, 0), axis=-1)
    dest = first + eidx[:, 2:4, :]
    n_blk = -(-n_assign // rows) + n_e
    blk_row0 = jnp.arange(n_blk, dtype=jnp.int32) * rows
    blk_e = jnp.minimum(jnp.sum(blk_row0[:, None] >= pends[None, :], axis=1), n_e - 1).astype(jnp.int32)
    n_used = (pends[-1] // rows).astype(jnp.int32).reshape(1)
    pos = dest.astype(jnp.int32).reshape(n_tiles, 1, 2 * tm)
    tails = jnp.concatenate([pstart + counts, pcounts - counts, n_used]).astype(jnp.int32)
    return blk_e, n_used, pos, n_blk, tails


def kernel(x, c, ctx, c_ctx, w_ada, b_ada, norm1_w, norm2_w, w_in, w_pool_group, pool_scale, hg_norm_w,
           lb_logits, w_branch_pool, w_branch_hgrn, w_out, w_router, router_bias, w_e1, w_e3, w_e2,
           final_norm_w):
    n_batch, n_lat, d = x.shape
    n_ctx = ctx.shape[1]
    depth = w_ada.shape[0]
    hw = hg_norm_w.shape[1]
    n_e = w_router.shape[1]
    t_sample = n_ctx + n_lat
    ntot = n_batch * t_sample
    ctx_row = n_batch
    assert n_batch < 8 and n_ctx == ROW_TILE and n_lat % ROW_TILE == 0

    cc = jnp.zeros((8, d), F32).at[:n_batch].set(c).at[ctx_row].set(c_ctx)
    mods = _ada_mods(cc, w_ada, b_ada)
    lb = _lower_bounds(lb_logits)
    w_router_t = jnp.transpose(w_router.astype(F32))

    xs, h = _norm_mod_rows(x, ctx, mods[0], norm1_w[0])
    for l in range(depth):
        z = _in_proj(h, w_in, l)
        o_f, o_b = _scan(z, lb, l, depth, n_batch, n_ctx, n_lat, hw)
        xs, h2, eidx, gate, counts = _merge(
            xs, z, o_f, o_b, mods[l], norm2_w[l], pool_scale[l], hg_norm_w[l], w_pool_group[l].astype(BF16),
            w_branch_pool[l].astype(BF16), w_branch_hgrn[l].astype(BF16), w_out[l].astype(BF16),
            w_router_t, router_bias, n_ctx, t_sample, ctx_row)
        blk_e, n_used, pos, n_blk, tails = _dispatch_plan(eidx, counts[:, 0], n_e)
        yb = _experts(_dispatch(h2, pos, tails, n_blk), blk_e, n_used, w_e1, w_e3, w_e2, l)
        g = jnp.transpose(gate[:, 0:2, :], (0, 2, 1)).reshape(ntot, 2)
        if l + 1 < depth:
            xs, h = _combine(xs, yb, pos, g, mods[l], n_batch, n_ctx, n_lat,
                             mods_next=mods[l + 1], norm_next=norm1_w[l + 1])
        else:
            out = _combine(xs, yb, pos, g, mods[l], n_batch, n_ctx, n_lat, final_w=final_norm_w)
    return out.reshape(n_batch, n_lat, d)
```

```python
import functools
import math

import numpy as np
import jax
import jax.numpy as jnp
from jax import lax
from jax.experimental import pallas as pl
from jax.experimental.pallas import tpu as pltpu

F32 = jnp.float32
BF16 = jnp.bfloat16

EPS = 1e-6
GRID_W = 64
POOL_WINDOWS = (2, 4, 8, 16)
HEAD_DIM = 128
N_EXPERT_GROUPS = 4
EXPERTS_PER_GROUP = 4
SCAN_CHUNK = 64
SCAN_GROUP = 4
ROW_TILE = 256
MOE_ROWS = 512
V7X_VMEM_BYTES = 64 * 1024 * 1024
VMEM_LIMIT = V7X_VMEM_BYTES - 8 * 1024 * 1024


def _cparams(n_axes, vmem=VMEM_LIMIT):
    return pltpu.CompilerParams(dimension_semantics=("arbitrary",) * n_axes, vmem_limit_bytes=vmem)


def _sigmoid(x):
    return 1.0 / (1.0 + jnp.exp(-x))


def _dot(a, b):
    return jnp.dot(a, b, preferred_element_type=F32)


def _dot_nt(a, b):
    return lax.dot_general(a, b, (((1,), (1,)), ((), ())), preferred_element_type=F32)


def _pack_halves(x):
    half = x.shape[1] // 2
    lo = lax.bitcast_convert_type(x[:, :half].astype(BF16).astype(F32), jnp.uint32)
    hi = lax.bitcast_convert_type(x[:, half:].astype(BF16).astype(F32), jnp.uint32)
    return (lo >> 16) | (hi & jnp.uint32(0xFFFF0000))


def _unpack_halves(w):
    lo = lax.bitcast_convert_type(w << 16, F32)
    hi = lax.bitcast_convert_type(w & jnp.uint32(0xFFFF0000), F32)
    return lo, hi


def _lb_kernel(depth, logit_ref, lb_ref):
    for d in range(2):
        rows = [logit_ref[d * depth + i:d * depth + i + 1, :] for i in range(depth)]
        m = rows[0]
        for r in rows[1:]:
            m = jnp.maximum(m, r)
        es = [jnp.exp(r - m) for r in rows]
        tot = es[0]
        for e in es[1:]:
            tot = tot + e
        ws = [e / tot for e in es]
        acc = jnp.zeros_like(ws[0])
        for i in range(depth):
            acc = acc + ws[i]
            lb_ref[d * depth + i:d * depth + i + 1, :] = acc - ws[0]


def _lower_bounds(lb_logits):
    _, depth, hw = lb_logits.shape
    flat = lb_logits.astype(F32).reshape(2 * depth, hw)
    out = pl.pallas_call(
        functools.partial(_lb_kernel, depth),
        name="lower_bounds",
        out_shape=jax.ShapeDtypeStruct((2 * depth, hw), F32),
    )(flat)
    return out.reshape(2 * depth, 1, hw)


def _ada_kernel(c_ref, w_ref, b_ref, o_ref):
    c = c_ref[...]
    s = (c * _sigmoid(c)).astype(BF16)
    o_ref[0] = _dot(s, w_ref[0].astype(BF16)) + b_ref[0]


def _ada_mods(cc, w_ada, b_ada):
    depth, d, n6 = w_ada.shape
    tn = 1024
    return pl.pallas_call(
        _ada_kernel,
        name="ada_mods",
        grid=(depth, n6 // tn),
        in_specs=[
            pl.BlockSpec((8, d), lambda l, j: (0, 0)),
            pl.BlockSpec((1, d, tn), lambda l, j: (l, 0, j)),
            pl.BlockSpec((1, 1, tn), lambda l, j: (l, 0, j)),
        ],
        out_specs=pl.BlockSpec((1, 8, tn), lambda l, j: (l, 0, j)),
        out_shape=jax.ShapeDtypeStruct((depth, 8, n6), F32),
        compiler_params=_cparams(2),
    )(cc, w_ada, b_ada.reshape(depth, 1, n6))


INPROJ_MAX_ROWS = 2176
INPROJ_COLS = 1024


def _mod_row(i, tiles_per_sample, ctx_row):
    return jnp.where(i % tiles_per_sample == 0, ctx_row, i // tiles_per_sample)


def _norm_mod(x, nw, shift, scale):
    n = x * lax.rsqrt(jnp.mean(x * x, axis=-1, keepdims=True) + EPS) * nw
    return n * (1.0 + scale) + shift


def _normmod_kernel(d, tiles_per_sample, ctx_row, x_ref, c_ref, mod_ref, nw_ref, xs_ref, h_ref):
    i = pl.program_id(0)
    mrow = _mod_row(i, tiles_per_sample, ctx_row)
    shift = mod_ref[pl.ds(mrow, 1), 0:d]
    scale = mod_ref[pl.ds(mrow, 1), d:2 * d]

    def emit(src_ref):
        x = src_ref[...]
        xs_ref[...] = x
        h_ref[...] = _norm_mod(x, nw_ref[...], shift, scale).astype(BF16)

    is_ctx = i % tiles_per_sample == 0
    pl.when(is_ctx)(lambda: emit(c_ref))
    pl.when(jnp.logical_not(is_ctx))(lambda: emit(x_ref))


def _norm_mod_rows(x, ctx, mods_l, norm_w):
    n_batch, n_lat, d = x.shape
    n_ctx = ctx.shape[1]
    tm = ROW_TILE
    assert n_ctx == tm
    lps = n_lat // tm
    tps = lps + 1
    ntot = n_batch * (n_ctx + n_lat)
    return pl.pallas_call(
        functools.partial(_normmod_kernel, d, tps, n_batch),
        name="norm_mod",
        grid=(n_batch * tps,),
        in_specs=[
            pl.BlockSpec((tm, d), lambda i: ((i // tps) * lps + jnp.maximum(i % tps - 1, 0), 0)),
            pl.BlockSpec((tm, d), lambda i: (i // tps, 0)),
            pl.BlockSpec((8, 2 * d), lambda i: (0, 0)),
            pl.BlockSpec((1, d), lambda i: (0, 0)),
        ],
        out_specs=[pl.BlockSpec((tm, d), lambda i: (i, 0))] * 2,
        out_shape=[jax.ShapeDtypeStruct((ntot, d), F32), jax.ShapeDtypeStruct((ntot, d), BF16)],
        compiler_params=_cparams(1),
    )(x.reshape(n_batch * n_lat, d), ctx.reshape(n_batch * n_ctx, d), mods_l, norm_w.reshape(1, d))


def _inproj_kernel(h_ref, w_ref, o_ref):
    o_ref[...] = _dot(h_ref[...], w_ref[0].astype(BF16)).astype(BF16)


def _in_proj(h, w_in, layer):
    ntot, d = h.shape
    n_cols = w_in.shape[2]
    tn = INPROJ_COLS
    tm = max(k for k in range(16, INPROJ_MAX_ROWS + 1, 16) if ntot % k == 0)
    return pl.pallas_call(
        _inproj_kernel,
        name="in_proj",
        grid=(ntot // tm, n_cols // tn),
        in_specs=[
            pl.BlockSpec((tm, d), lambda r, j: (r, 0)),
            pl.BlockSpec((1, d, tn), lambda r, j: (layer, 0, j)),
        ],
        out_specs=pl.BlockSpec((tm, tn), lambda r, j: (r, j)),
        out_shape=jax.ShapeDtypeStruct((ntot, n_cols), BF16),
        compiler_params=_cparams(2),
    )(h, w_in)


COMPACT_MIN_HALF = 8
MXU_LEVEL_MIN_HALF = 2


def _scan_levels(c):
    out, m = [], c // 2
    while m >= MXU_LEVEL_MIN_HALF:
        out.append(m)
        m //= 2
    return out


def _scan_row_layout(c):
    off = {"b": 0, "kend": c, "tot": 2 * c}
    pos = 2 * c + 8
    for m in _scan_levels(c):
        n = c // 2 if m >= COMPACT_MIN_HALF else c
        off[("q", m)] = pos
        off[("k", m)] = pos + n
        pos += 2 * n
    return off, pos


def _scan_constants(c):
    t = np.arange(c)[:, None]
    r = np.arange(c)[None, :]
    out = []
    for flip in (False, True):
        fl = (lambda a: a[::-1, ::-1]) if flip else (lambda a: a)
        sets = [fl(r <= t), fl(r > t), np.ones((8, c), bool)]
        for m in _scan_levels(c):
            mid = (t // (2 * m)) * (2 * m) + m
            qa = fl((t >= mid) & (r >= mid) & (r <= t))
            ka = fl((t < mid) & (r > t) & (r <= mid - 1))
            if m >= COMPACT_MIN_HALF:
                rows = np.arange(c)
                q_valid = (rows % (2 * m) >= m) != flip
                sets += [qa[q_valid], ka[~q_valid]]
            else:
                sets += [qa, ka]
        mat = np.concatenate(sets, axis=0).astype(np.float32)
        out.append(np.concatenate([mat, mat], axis=1))
    return np.stack(out)


def _tile_pairs(e, m, c):
    parts = []
    for p in range(c // (2 * m)):
        blk = e[p * m:(p + 1) * m]
        parts += [blk, blk]
    return jnp.concatenate(parts, axis=0)


def _scan_gates(c, q_ref, f_ref, lb_ref, mat, e_ref):
    zq = q_ref[...].astype(F32)
    zf = f_ref[...].astype(F32)
    lbv = lb_ref[0]
    f = lbv + (1.0 - lbv) * _sigmoid(zf)
    kk = 1.0 - f
    qq = zq * _sigmoid(zq)
    lf2 = jnp.log(f) * math.log2(math.e)
    lf_hi = lf2.astype(BF16)
    lf_lo = (lf2 - lf_hi.astype(F32)).astype(BF16)
    e_ref[...] = jnp.exp2(_dot(mat, jnp.concatenate([lf_hi, lf_lo], axis=0)))
    row = lax.broadcasted_iota(jnp.int32, (c, 1), 0)
    k_sw = jnp.where((row & 1) == 1, pltpu.roll(kk, 1, axis=0), pltpu.roll(kk, c - 1, axis=0))
    return qq, kk, qq * f, k_sw


def _scan_intra(n_heads, c, backward, qq, kk, qf, k_sw, e_ref, a_ref):
    levels = _scan_levels(c)
    off, _ = _scan_row_layout(c)
    ti = lax.broadcasted_iota(jnp.int32, (c, c), 0)
    si = lax.broadcasted_iota(jnp.int32, (c, c), 1)
    same = {m: (ti & -(2 * m)) == (si & -(2 * m)) for m in levels[1:] + [1]}
    eye = ti == si
    causal = (ti <= si) if backward else (ti >= si)
    for h in range(n_heads):
        cs = slice(h * HEAD_DIM, (h + 1) * HEAD_DIM)
        qh = qq[:, cs]
        kh = kk[:, cs]
        a = None
        for m in levels:
            n = c // 2 if m >= COMPACT_MIN_HALF else c
            eq = e_ref[off[("q", m)]:off[("q", m)] + n, cs]
            ek = e_ref[off[("k", m)]:off[("k", m)] + n, cs]
            if m >= COMPACT_MIN_HALF:
                eq, ek = _tile_pairs(eq, m, c), _tile_pairs(ek, m, c)
            p = _dot_nt((qh * eq).astype(BF16), (kh * ek).astype(BF16))
            a = p if a is None else jnp.where(same[m], p, a)
        pair = jnp.sum(qf[:, cs] * k_sw[:, cs], axis=-1, keepdims=True)
        diag = jnp.sum(qh * kh, axis=-1, keepdims=True)
        a = jnp.where(same[1], pair, a)
        a = jnp.where(eye, diag, a)
        a_ref[h] = jnp.where(causal, a, 0.0).astype(BF16)


def _scan_state(n_heads, c, qq, kk, i_ref, e_ref, a_ref, s_ref, o_ref):
    off, _ = _scan_row_layout(c)
    for h in range(n_heads):
        cs = slice(h * HEAD_DIM, (h + 1) * HEAD_DIM)
        qh = qq[:, cs]
        kh = kk[:, cs]
        vh = i_ref[:, cs]
        st = s_ref[h]
        qb = (qh * e_ref[off["b"]:off["b"] + c, cs]).astype(BF16)
        o = _dot_nt(qb, st.astype(BF16)) + _dot(a_ref[h], vh)
        o_ref[:, cs] = o.astype(BF16)
        vt = vh.astype(F32).T.astype(BF16)
        kt = (kh * e_ref[off["kend"]:off["kend"] + c, cs]).astype(BF16)
        s_ref[h] = st * e_ref[off["tot"]:off["tot"] + 1, cs] + _dot(vt, kt)


def _scan_kernel(n_heads, c, qf_ref, qb_ref, ff_ref, fb_ref, if_ref, ib_ref, lbf_ref, lbb_ref, m_ref,
                 of_ref, ob_ref, s_ref, e_ref, a_ref):
    @pl.when(pl.program_id(1) == 0)
    def _():
        s_ref[...] = jnp.zeros_like(s_ref)

    dirs = ((qf_ref, ff_ref, if_ref, lbf_ref, of_ref), (qb_ref, fb_ref, ib_ref, lbb_ref, ob_ref))
    work = [(d, pl.ds((SCAN_GROUP - 1 - k if d else k) * c, c)) for d in range(2) for k in range(SCAN_GROUP)]
    gates = []
    for n, (d, rows) in enumerate(work):
        q, f, _, lb, _ = dirs[d]
        gates.append(_scan_gates(c, q.at[rows], f.at[rows], lb, m_ref[d], e_ref.at[n]))
    for n, (d, rows) in enumerate(work):
        _scan_intra(n_heads, c, d == 1, *gates[n], e_ref.at[n], a_ref.at[n])
    for n, (d, rows) in enumerate(work):
        _, _, i_ref, _, o_ref = dirs[d]
        _scan_state(n_heads, c, gates[n][0], gates[n][1], i_ref.at[rows], e_ref.at[n], a_ref.at[n],
                    s_ref.at[d], o_ref.at[rows])


def _scan(z, lb, layer, depth, n_batch, n_ctx, n_lat, hw):
    c = SCAN_CHUNK
    blk = SCAN_GROUP * c
    n_heads = hw // HEAD_DIM
    assert n_ctx % blk == 0 and n_lat % blk == 0
    nc, nl = n_ctx // blk, n_lat // blk
    steps = nc + nl
    mats = jnp.asarray(_scan_constants(c), BF16)
    n_rows = mats.shape[1]
    assert n_rows == _scan_row_layout(c)[1]
    ntot = z.shape[0]

    def fwd(b, s):
        return b * steps + s

    def bwd(b, s):
        return b * steps + jnp.where(s < nc, nc - 1 - s, nc + (steps - 1 - s))

    def col(chunk, j):
        return pl.BlockSpec((blk, hw), lambda b, s: (chunk(b, s), j))

    lb_spec = lambda d: pl.BlockSpec((1, 1, hw), lambda b, s: (d * depth + layer, 0, 0))
    return pl.pallas_call(
        functools.partial(_scan_kernel, n_heads, c),
        name="hgrn2_scan",
        grid=(n_batch, steps),
        in_specs=[
            col(fwd, 1), col(bwd, 1),
            col(fwd, 2), col(bwd, 3),
            col(fwd, 4), col(bwd, 4),
            lb_spec(0), lb_spec(1),
            pl.BlockSpec((2, n_rows, 2 * c), lambda b, s: (0, 0, 0)),
        ],
        out_specs=[pl.BlockSpec((blk, hw), lambda b, s: (fwd(b, s), 0)),
                   pl.BlockSpec((blk, hw), lambda b, s: (bwd(b, s), 0))],
        out_shape=[jax.ShapeDtypeStruct((ntot, hw), BF16)] * 2,
        scratch_shapes=[pltpu.VMEM((2, n_heads, HEAD_DIM, HEAD_DIM), F32),
                        pltpu.VMEM((2 * SCAN_GROUP, n_rows, hw), F32),
                        pltpu.VMEM((2 * SCAN_GROUP, n_heads, c, c), BF16)],
        compiler_params=_cparams(2),
    )(z, z, z, z, z, z, lb, lb, mats)


def _pool_constants(tile, n_ctx):
    assert n_ctx == tile and tile % GRID_W == 0
    band = np.zeros((2, len(POOL_WINDOWS), tile, tile), np.float32)
    inv = np.zeros((2, len(POOL_WINDOWS), tile, 1), np.float32)
    for kind, row_len in enumerate((n_ctx, GRID_W)):
        t = np.arange(tile)
        tau = t % row_len
        base = t - tau
        for g, w in enumerate(POOL_WINDOWS):
            lo = np.maximum(tau - w // 2, 0)
            hi = np.minimum(tau + w // 2 - 1, row_len - 1)
            s = np.arange(tile)[None, :]
            band[kind, g] = (s >= (base + lo)[:, None]) & (s <= (base + hi)[:, None])
            inv[kind, g, :, 0] = 1.0 / (hi - lo + 1)
    return band, inv


def _route(sel, sc):
    n_g, per = N_EXPERT_GROUPS, EXPERTS_PER_GROUP
    row = lambda a, e: a[e:e + 1, :]
    best = None
    for g in range(n_g):
        rows = [row(sel, g * per + i) for i in range(per)]
        gs = None
        for i in range(per):
            for j in range(i + 1, per):
                p = rows[i] + rows[j]
                gs = p if gs is None else jnp.maximum(gs, p)
        if best is None:
            best, g_idx = gs, jnp.zeros(gs.shape, jnp.int32)
        else:
            upd = gs > best
            g_idx = jnp.where(upd, g, g_idx)
            best = jnp.where(upd, gs, best)
    cand, raw = [], []
    for i in range(per):
        ci, ri = row(sel, i), row(sc, i)
        for g in range(1, n_g):
            ci = jnp.where(g_idx == g, row(sel, g * per + i), ci)
            ri = jnp.where(g_idx == g, row(sc, g * per + i), ri)
        cand.append(ci)
        raw.append(ri)

    def argmax4(vals):
        m, loc = vals[0], jnp.zeros(vals[0].shape, jnp.int32)
        for i in range(1, per):
            upd = vals[i] > m
            loc = jnp.where(upd, i, loc)
            m = jnp.where(upd, vals[i], m)
        return loc

    loc0 = argmax4(cand)
    loc1 = argmax4([jnp.where(loc0 == i, -jnp.inf, cand[i]) for i in range(per)])

    def pick(loc):
        out = raw[0]
        for i in range(1, per):
            out = jnp.where(loc == i, raw[i], out)
        return out

    g0, g1 = pick(loc0), pick(loc1)
    tot = g0 + g1
    return g_idx * per + loc0, g_idx * per + loc1, g0 / tot, g1 / tot


def _merge_kernel(d, hw, tiles_per_sample, ctx_row,
                  x_ref, p_ref, og_ref, gp_ref, gh_ref, of_ref, ob_ref, mod_ref, n2_ref, ps_ref, hn_ref,
                  band_ref, inv_ref, wpg_ref, wbp_ref, wbh_ref, wout_ref, wr_ref, rb_ref, tri_ref,
                  xo_ref, h2_ref, eidx_ref, gate_ref, cnt_ref, pool_scr, hg_scr, carry_ref):
    i = pl.program_id(0)
    n_heads = hw // HEAD_DIM
    n_groups = len(POOL_WINDOWS)
    pg = p_ref.shape[1] // n_groups
    mrow = _mod_row(i, tiles_per_sample, ctx_row)
    mod = lambda k: mod_ref[pl.ds(mrow, 1), k * d:(k + 1) * d]

    for h in range(n_heads):
        cs = slice(h * HEAD_DIM, (h + 1) * HEAD_DIM)
        o = of_ref[:, cs].astype(F32) + ob_ref[:, cs].astype(F32)
        o = o * lax.rsqrt(jnp.mean(o * o, axis=-1, keepdims=True) + EPS)
        og = og_ref[:, cs].astype(F32)
        hg_scr[:, cs] = (o * hn_ref[:, cs] * (og * _sigmoid(og))).astype(BF16)

    for g in range(n_groups):
        cs = slice(g * pg, (g + 1) * pg)
        u = p_ref[:, cs]
        wsum = _dot(band_ref[0, g], u)
        dlt = wsum * inv_ref[0, g] - u.astype(F32)
        pool_scr[:, cs] = (_dot(dlt.astype(BF16), wpg_ref[g]) * ps_ref[:, cs]).astype(BF16)

    gp = gp_ref[...].astype(F32)
    gh = gh_ref[...].astype(F32)
    m = _sigmoid(gp) * _dot(pool_scr[...], wbp_ref[...]) + _sigmoid(gh) * _dot(hg_scr[...], wbh_ref[...])
    y = _dot(m.astype(BF16), wout_ref[...])
    xn = x_ref[...] + mod(2) * y
    xo_ref[...] = xn
    n2 = xn * lax.rsqrt(jnp.mean(xn * xn, axis=-1, keepdims=True) + EPS) * n2_ref[...]
    h2 = n2 * (1.0 + mod(4)) + mod(3)
    h2_ref[...] = _pack_halves(h2)

    sc = _sigmoid(_dot_nt(wr_ref[...].astype(BF16), h2.astype(BF16)))
    e0, e1, g0, g1 = _route(sc + rb_ref[...], sc)

    @pl.when(i == 0)
    def _():
        carry_ref[...] = jnp.zeros_like(carry_ref)

    er = lax.broadcasted_iota(jnp.int32, sc.shape, 0)
    oh0 = (er == e0).astype(F32)
    oh1 = (er == e1).astype(F32)
    both = oh0 + oh1
    base = carry_ref[:, 0:1] + _dot(both.astype(BF16), tri_ref[...])
    r0 = jnp.sum(oh0 * base, axis=0, keepdims=True)
    r1 = jnp.sum(oh1 * base, axis=0, keepdims=True)
    carry_ref[...] = carry_ref[...] + jnp.sum(both, axis=1, keepdims=True)
    cnt_ref[...] = carry_ref[...].astype(jnp.int32)

    eidx_ref[...] = jnp.zeros_like(eidx_ref)
    gate_ref[...] = jnp.zeros_like(gate_ref)
    eidx_ref[0, 0:1, :] = e0
    eidx_ref[0, 1:2, :] = e1
    eidx_ref[0, 2:3, :] = r0.astype(jnp.int32)
    eidx_ref[0, 3:4, :] = r1.astype(jnp.int32)
    gate_ref[0, 0:1, :] = g0
    gate_ref[0, 1:2, :] = g1


def _const_spec(shape):
    nd = len(shape)
    return pl.BlockSpec(shape, lambda i: (0,) * nd, pipeline_mode=pl.Buffered(1))


def _merge(x, z, o_f, o_b, mods_l, norm2_w, pool_scale, hg_norm_w, w_pg, w_bp, w_bh, w_out, w_router_t,
           router_bias, n_ctx, t_sample, ctx_row):
    ntot, d = x.shape
    hw = o_f.shape[1]
    d_pool = w_bp.shape[0]
    assert d_pool == hw and d == 2 * hw
    tm = ROW_TILE
    tps = t_sample // tm
    n_e = w_router_t.shape[0]
    band, inv = _pool_constants(tm, n_ctx)
    band = jnp.asarray(band, BF16)
    inv = jnp.asarray(inv, F32)
    kind = lambda i: jnp.where(i % tps == 0, 0, 1)
    n_tiles = ntot // tm
    before = jnp.asarray(np.triu(np.ones((tm, tm), np.float32), 1), BF16)
    return pl.pallas_call(
        functools.partial(_merge_kernel, d, hw, tps, ctx_row),
        name="merge",
        grid=(n_tiles,),
        in_specs=[
            pl.BlockSpec((tm, d), lambda s: (s, 0)),
            pl.BlockSpec((tm, hw), lambda s: (s, 0)),
            pl.BlockSpec((tm, hw), lambda s: (s, 5)),
            pl.BlockSpec((tm, d), lambda s: (s, 3)),
            pl.BlockSpec((tm, d), lambda s: (s, 4)),
            pl.BlockSpec((tm, hw), lambda s: (s, 0)),
            pl.BlockSpec((tm, hw), lambda s: (s, 0)),
            _const_spec((8, 6 * d)),
            _const_spec((1, d)),
            _const_spec((1, hw)),
            _const_spec((1, hw)),
            pl.BlockSpec((1,) + band.shape[1:], lambda s: (kind(s), 0, 0, 0)),
            pl.BlockSpec((1,) + inv.shape[1:], lambda s: (kind(s), 0, 0, 0)),
            _const_spec(w_pg.shape),
            _const_spec(w_bp.shape),
            _const_spec(w_bh.shape),
            _const_spec(w_out.shape),
            _const_spec(w_router_t.shape),
            _const_spec((n_e, 1)),
            _const_spec((tm, tm)),
        ],
        out_specs=[
            pl.BlockSpec((tm, d), lambda s: (s, 0)),
            pl.BlockSpec((tm, d // 2), lambda s: (s, 0)),
            pl.BlockSpec((1, 8, tm), lambda s: (s, 0, 0)),
            pl.BlockSpec((1, 8, tm), lambda s: (s, 0, 0)),
            pl.BlockSpec((n_e, 128), lambda s: (0, 0)),
        ],
        out_shape=[
            jax.ShapeDtypeStruct((ntot, d), F32),
            jax.ShapeDtypeStruct((ntot, d // 2), jnp.uint32),
            jax.ShapeDtypeStruct((n_tiles, 8, tm), jnp.int32),
            jax.ShapeDtypeStruct((n_tiles, 8, tm), F32),
            jax.ShapeDtypeStruct((n_e, 128), jnp.int32),
        ],
        scratch_shapes=[pltpu.VMEM((tm, hw), BF16), pltpu.VMEM((tm, hw), BF16), pltpu.VMEM((n_e, 128), F32)],
        compiler_params=_cparams(1),
    )(x, z, z, z, z, o_f, o_b, mods_l, norm2_w.reshape(1, d), pool_scale.reshape(1, hw),
      hg_norm_w.reshape(1, hw), band, inv, w_pg, w_bp, w_bh, w_out, w_router_t,
      router_bias.astype(F32).reshape(n_e, 1), before)


def _row_gather_start(src_hbm, idx_ref, dst, sem, n_rows):
    def body(it, carry):
        for k in range(2):
            r = 2 * it + k
            tok = idx_ref[0, 0, r]
            pltpu.make_async_copy(src_hbm.at[pl.ds(tok, 1)], dst.at[pl.ds(r, 1)], sem).start(priority=k)
        return carry

    lax.fori_loop(0, n_rows // 2, body, 0, unroll=8)


def _row_gather_wait(src_hbm, dst, sem, n_rows):
    pltpu.make_async_copy(src_hbm.at[pl.ds(0, n_rows)], dst, sem).wait()


def _dispatch_kernel(n_e, n_blk, tails_ref, pos_ref, h_ref, xs_hbm, zblk, sems):
    tm = h_ref.shape[0]
    rows = zblk.shape[0]
    sem = sems.at[0]
    zsem = sems.at[1]

    def zero_block(b):
        return pltpu.make_async_copy(zblk, xs_hbm.at[pl.ds(pl.multiple_of(b * rows, rows), rows)], zsem)

    def for_each_zero_block(act):
        for e in range(n_e):
            pl.when(tails_ref[e] >= 0)(lambda e=e: act(zero_block(tails_ref[e])))

        def blk_body(b, carry):
            act(zero_block(b))
            return carry

        lax.fori_loop(tails_ref[n_e], n_blk, blk_body, 0)

    @pl.when(pl.program_id(0) == 0)
    def _():
        zblk[...] = jnp.zeros_like(zblk)
        for_each_zero_block(lambda cp: cp.start())
        for_each_zero_block(lambda cp: cp.wait())

    def body(r, carry):
        for k in range(2):
            dst = pos_ref[0, 0, k * tm + r]
            pltpu.make_async_copy(h_ref.at[pl.ds(r, 1)], xs_hbm.at[pl.ds(dst, 1)], sem).start(priority=k)
        return carry

    lax.fori_loop(0, tm, body, 0, unroll=8)
    pltpu.make_async_copy(xs_hbm.at[pl.ds(0, 2 * tm)], xs_hbm.at[pl.ds(0, 2 * tm)], sem).wait()


def _dispatch(h2, pos, tails, n_blk):
    ntot, d = h2.shape
    tm = ROW_TILE
    n_tiles = ntot // tm
    n_e = tails.shape[0] - 1
    grid_spec = pltpu.PrefetchScalarGridSpec(
        num_scalar_prefetch=1,
        grid=(n_tiles,),
        in_specs=[
            pl.BlockSpec((1, 1, 2 * tm), lambda i, t: (i, 0, 0), memory_space=pltpu.SMEM),
            pl.BlockSpec((tm, d), lambda i, t: (i, 0)),
        ],
        out_specs=pl.BlockSpec(memory_space=pl.ANY),
        scratch_shapes=[pltpu.VMEM((MOE_ROWS, d), h2.dtype), pltpu.SemaphoreType.DMA((2,))],
    )
    return pl.pallas_call(
        functools.partial(_dispatch_kernel, n_e, n_blk),
        name="dispatch",
        grid_spec=grid_spec,
        out_shape=jax.ShapeDtypeStruct((n_blk * MOE_ROWS, d), h2.dtype),
        compiler_params=_cparams(1),
    )(tails, pos, h2)


def _expert_kernel(blk_e_ref, nused_ref, x_ref, w1_ref, w3_ref, w2_ref, y_ref, w1c, w3c, w2c):
    i = pl.program_id(0)
    n_used = nused_ref[0]

    @pl.when(i < n_used)
    def _():
        changed = jnp.logical_or(i == 0, blk_e_ref[i] != blk_e_ref[jnp.maximum(i - 1, 0)])

        @pl.when(changed)
        def _():
            w1c[...] = w1_ref[0, 0].astype(BF16)
            w3c[...] = w3_ref[0, 0].astype(BF16)
            w2c[...] = w2_ref[0, 0].astype(BF16)

        xb = jnp.concatenate(_unpack_halves(x_ref[...]), axis=1).astype(BF16)
        h1 = _dot(xb, w1c[...])
        h3 = _dot(xb, w3c[...])
        act = (h1 * _sigmoid(h1) * h3).astype(BF16)
        y_ref[...] = _pack_halves(_dot(act, w2c[...]))

    @pl.when(i >= n_used)
    def _():
        y_ref[...] = jnp.zeros_like(y_ref)


def _experts(xs, blk_e, n_used, w1, w3, w2, layer):
    n_rows = xs.shape[0]
    _, n_e, d, f = w1.shape
    n_blk = blk_e.shape[0]
    rows = MOE_ROWS
    used = lambda i, nu: jnp.minimum(i, nu[0] - 1)
    grid_spec = pltpu.PrefetchScalarGridSpec(
        num_scalar_prefetch=2,
        grid=(n_blk,),
        in_specs=[
            pl.BlockSpec((rows, d // 2), lambda i, be, nu: (used(i, nu), 0)),
            pl.BlockSpec((1, 1, d, f), lambda i, be, nu: (layer, be[i], 0, 0)),
            pl.BlockSpec((1, 1, d, f), lambda i, be, nu: (layer, be[i], 0, 0)),
            pl.BlockSpec((1, 1, f, d), lambda i, be, nu: (layer, be[i], 0, 0)),
        ],
        out_specs=pl.BlockSpec((rows, d // 2), lambda i, be, nu: (i, 0)),
        scratch_shapes=[
            pltpu.VMEM((d, f), BF16),
            pltpu.VMEM((d, f), BF16),
            pltpu.VMEM((f, d), BF16),
        ],
    )
    return pl.pallas_call(
        _expert_kernel,
        name="experts",
        grid_spec=grid_spec,
        out_shape=jax.ShapeDtypeStruct((n_rows, d // 2), jnp.uint32),
        compiler_params=_cparams(1),
    )(blk_e, n_used, xs, w1, w3, w2)


def _combine_kernel(d, mode, tile_of, pos0_ref, posn_ref, y_hbm, x_ref, g_ref, mod_ref, *rest):
    if mode == "next":
        modn_ref, nw_ref, xo_ref, h_ref, rbuf, sem = rest
    else:
        fw_ref, out_ref, rbuf, sem = rest
    i = pl.program_id(0)
    n = pl.num_programs(0)
    tm = x_ref.shape[0]
    slot = i % 2

    @pl.when(i == 0)
    def _():
        _row_gather_start(y_hbm, pos0_ref, rbuf.at[0], sem.at[0], 2 * tm)

    @pl.when(i + 1 < n)
    def _():
        _row_gather_start(y_hbm, posn_ref, rbuf.at[1 - slot], sem.at[1 - slot], 2 * tm)

    _row_gather_wait(y_hbm, rbuf.at[slot], sem.at[slot], 2 * tm)
    mrow = tile_of(i)[1]
    m5 = mod_ref[pl.ds(mrow, 1), 5 * d:6 * d]
    g = g_ref[...]
    lo0, hi0 = _unpack_halves(rbuf[slot, 0:tm, :])
    lo1, hi1 = _unpack_halves(rbuf[slot, tm:2 * tm, :])
    g0, g1 = g[:, 0:1], g[:, 1:2]
    f = jnp.concatenate([g0 * lo0 + g1 * lo1, g0 * hi0 + g1 * hi1], axis=1)
    xn = x_ref[...] + m5 * f
    if mode == "next":
        xo_ref[...] = xn
        shift = modn_ref[pl.ds(mrow, 1), 0:d]
        scale = modn_ref[pl.ds(mrow, 1), d:2 * d]
        h_ref[...] = _norm_mod(xn, nw_ref[...], shift, scale).astype(BF16)
    else:
        out_ref[...] = xn * lax.rsqrt(jnp.mean(xn * xn, axis=-1, keepdims=True) + EPS) * fw_ref[...]


def _combine(x, yb, pos, gates, mods_l, n_batch, n_ctx, n_lat, mods_next=None, norm_next=None, final_w=None):
    ntot, d = x.shape
    tm = ROW_TILE
    tps, cps, lps = (n_ctx + n_lat) // tm, n_ctx // tm, n_lat // tm
    mode = "final" if final_w is not None else "next"
    if mode == "final":
        n_steps = n_batch * lps

        def tile_of(i):
            return (i // lps) * tps + cps + i % lps, i // lps
    else:
        n_steps = ntot // tm

        def tile_of(i):
            return i, _mod_row(i, tps, n_batch)

    tile = lambda i: tile_of(i)[0]
    in_specs = [
        pl.BlockSpec((1, 1, 2 * tm), lambda i: (tile(0), 0, 0), memory_space=pltpu.SMEM),
        pl.BlockSpec((1, 1, 2 * tm), lambda i: (tile(jnp.minimum(i + 1, n_steps - 1)), 0, 0),
                     memory_space=pltpu.SMEM),
        pl.BlockSpec(memory_space=pl.ANY),
        pl.BlockSpec((tm, d), lambda i: (tile(i), 0)),
        pl.BlockSpec((tm, 2), lambda i: (tile(i), 0)),
        _const_spec((8, 6 * d)),
    ]
    args = [pos, pos, yb, x, gates, mods_l]
    if mode == "next":
        in_specs += [pl.BlockSpec((8, 2 * d), lambda i: (0, 0)), pl.BlockSpec((1, d), lambda i: (0, 0))]
        args += [mods_next, norm_next.reshape(1, d)]
        out_specs = [pl.BlockSpec((tm, d), lambda i: (i, 0))] * 2
        out_shape = [jax.ShapeDtypeStruct((ntot, d), F32), jax.ShapeDtypeStruct((ntot, d), BF16)]
    else:
        in_specs += [pl.BlockSpec((1, d), lambda i: (0, 0))]
        args += [final_w.reshape(1, d)]
        out_specs = pl.BlockSpec((tm, d), lambda i: (i, 0))
        out_shape = jax.ShapeDtypeStruct((n_steps * tm, d), F32)
    return pl.pallas_call(
        functools.partial(_combine_kernel, d, mode, tile_of),
        name="combine",
        grid=(n_steps,),
        in_specs=in_specs,
        out_specs=out_specs,
        out_shape=out_shape,
        scratch_shapes=[pltpu.VMEM((2, 2 * tm, d // 2), jnp.uint32), pltpu.SemaphoreType.DMA((2,))],
        compiler_params=_cparams(1),
    )(*args)


def _dispatch_plan(eidx, counts, n_e):
    rows = MOE_ROWS
    n_tiles, _, tm = eidx.shape
    n_assign = n_tiles * tm * 2
    pcounts = (counts + rows - 1) // rows * rows
    pends = jnp.cumsum(pcounts)
    pstart = pends - pcounts
    experts = jnp.arange(n_e, dtype=jnp.int32)
    first = jnp.sum(jnp.where(eidx[:, 0:2, :, None] == experts, pstart, 0), axis=-1)
    dest = first + eidx[:, 2:4, :]
    n_blk = -(-n_assign // rows) + n_e
    blk_row0 = jnp.arange(n_blk, dtype=jnp.int32) * rows
    blk_e = jnp.minimum(jnp.sum(blk_row0[:, None] >= pends[None, :], axis=1), n_e - 1).astype(jnp.int32)
    n_used = (pends[-1] // rows).astype(jnp.int32).reshape(1)
    pos = dest.astype(jnp.int32).reshape(n_tiles, 1, 2 * tm)
    tails = jnp.concatenate([jnp.where(pcounts > counts, pends // rows - 1, -1), n_used]).astype(jnp.int32)
    return blk_e, n_used, pos, n_blk, tails


def kernel(x, c, ctx, c_ctx, w_ada, b_ada, norm1_w, norm2_w, w_in, w_pool_group, pool_scale, hg_norm_w,
           lb_logits, w_branch_pool, w_branch_hgrn, w_out, w_router, router_bias, w_e1, w_e3, w_e2,
           final_norm_w):
    n_batch, n_lat, d = x.shape
    n_ctx = ctx.shape[1]
    depth = w_ada.shape[0]
    hw = hg_norm_w.shape[1]
    n_e = w_router.shape[1]
    t_sample = n_ctx + n_lat
    ntot = n_batch * t_sample
    ctx_row = n_batch
    assert n_batch < 8 and n_ctx == ROW_TILE and n_lat % ROW_TILE == 0

    cc = jnp.zeros((8, d), F32).at[:n_batch].set(c).at[ctx_row].set(c_ctx)
    mods = _ada_mods(cc, w_ada, b_ada)
    lb = _lower_bounds(lb_logits)
    w_router_t = jnp.transpose(w_router.astype(F32))

    xs, h = _norm_mod_rows(x, ctx, mods[0], norm1_w[0])
    for l in range(depth):
        z = _in_proj(h, w_in, l)
        o_f, o_b = _scan(z, lb, l, depth, n_batch, n_ctx, n_lat, hw)
        xs, h2, eidx, gate, counts = _merge(
            xs, z, o_f, o_b, mods[l], norm2_w[l], pool_scale[l], hg_norm_w[l], w_pool_group[l].astype(BF16),
            w_branch_pool[l].astype(BF16), w_branch_hgrn[l].astype(BF16), w_out[l].astype(BF16),
            w_router_t, router_bias, n_ctx, t_sample, ctx_row)
        blk_e, n_used, pos, n_blk, tails = _dispatch_plan(eidx, counts[:, 0], n_e)
        yb = _experts(_dispatch(h2, pos, tails, n_blk), blk_e, n_used, w_e1, w_e3, w_e2, l)
        g = jnp.transpose(gate[:, 0:2, :], (0, 2, 1)).reshape(ntot, 2)
        if l + 1 < depth:
            xs, h = _combine(xs, yb, pos, g, mods[l], n_batch, n_ctx, n_lat,
                             mods_next=mods[l + 1], norm_next=norm1_w[l + 1])
        else:
            out = _combine(xs, yb, pos, g, mods[l], n_batch, n_ctx, n_lat, final_w=final_norm_w)
    return out.reshape(n_batch, n_lat, d)
```

```python
import functools
import math

import numpy as np
import jax
import jax.numpy as jnp
from jax import lax
from jax.experimental import pallas as pl
from jax.experimental.pallas import tpu as pltpu

F32 = jnp.float32
BF16 = jnp.bfloat16

EPS = 1e-6
GRID_W = 64
POOL_WINDOWS = (2, 4, 8, 16)
HEAD_DIM = 128
N_EXPERT_GROUPS = 4
EXPERTS_PER_GROUP = 4
SCAN_CHUNK = 64
SCAN_GROUP = 4
ROW_TILE = 256
MOE_ROWS = 512
V7X_VMEM_BYTES = 64 * 1024 * 1024
VMEM_LIMIT = V7X_VMEM_BYTES - 8 * 1024 * 1024


def _cparams(n_axes, vmem=VMEM_LIMIT):
    return pltpu.CompilerParams(dimension_semantics=("arbitrary",) * n_axes, vmem_limit_bytes=vmem)


def _sigmoid(x):
    return 1.0 / (1.0 + jnp.exp(-x))


def _dot(a, b):
    return jnp.dot(a, b, preferred_element_type=F32)


def _dot_nt(a, b):
    return lax.dot_general(a, b, (((1,), (1,)), ((), ())), preferred_element_type=F32)


def _pack_halves(x):
    half = x.shape[1] // 2
    lo = lax.bitcast_convert_type(x[:, :half].astype(BF16).astype(F32), jnp.uint32)
    hi = lax.bitcast_convert_type(x[:, half:].astype(BF16).astype(F32), jnp.uint32)
    return (lo >> 16) | (hi & jnp.uint32(0xFFFF0000))


def _unpack_halves(w):
    lo = lax.bitcast_convert_type(w << 16, F32)
    hi = lax.bitcast_convert_type(w & jnp.uint32(0xFFFF0000), F32)
    return lo, hi


def _lb_kernel(depth, logit_ref, lb_ref):
    for d in range(2):
        rows = [logit_ref[d * depth + i:d * depth + i + 1, :] for i in range(depth)]
        m = rows[0]
        for r in rows[1:]:
            m = jnp.maximum(m, r)
        es = [jnp.exp(r - m) for r in rows]
        tot = es[0]
        for e in es[1:]:
            tot = tot + e
        ws = [e / tot for e in es]
        acc = jnp.zeros_like(ws[0])
        for i in range(depth):
            acc = acc + ws[i]
            lb_ref[d * depth + i:d * depth + i + 1, :] = acc - ws[0]


def _lower_bounds(lb_logits):
    _, depth, hw = lb_logits.shape
    flat = lb_logits.astype(F32).reshape(2 * depth, hw)
    out = pl.pallas_call(
        functools.partial(_lb_kernel, depth),
        name="lower_bounds",
        out_shape=jax.ShapeDtypeStruct((2 * depth, hw), F32),
    )(flat)
    return out.reshape(2 * depth, 1, hw)


def _ada_kernel(c_ref, w_ref, b_ref, o_ref):
    c = c_ref[...]
    s = (c * _sigmoid(c)).astype(BF16)
    o_ref[0] = _dot(s, w_ref[0].astype(BF16)) + b_ref[0]


def _ada_mods(cc, w_ada, b_ada):
    depth, d, n6 = w_ada.shape
    tn = 1024
    return pl.pallas_call(
        _ada_kernel,
        name="ada_mods",
        grid=(depth, n6 // tn),
        in_specs=[
            pl.BlockSpec((8, d), lambda l, j: (0, 0)),
            pl.BlockSpec((1, d, tn), lambda l, j: (l, 0, j)),
            pl.BlockSpec((1, 1, tn), lambda l, j: (l, 0, j)),
        ],
        out_specs=pl.BlockSpec((1, 8, tn), lambda l, j: (l, 0, j)),
        out_shape=jax.ShapeDtypeStruct((depth, 8, n6), F32),
        compiler_params=_cparams(2),
    )(cc, w_ada, b_ada.reshape(depth, 1, n6))


INPROJ_MAX_ROWS = 2176
INPROJ_COLS = 1024


def _mod_row(i, tiles_per_sample, ctx_row):
    return jnp.where(i % tiles_per_sample == 0, ctx_row, i // tiles_per_sample)


def _norm_mod(x, nw, shift, scale):
    n = x * lax.rsqrt(jnp.mean(x * x, axis=-1, keepdims=True) + EPS) * nw
    return n * (1.0 + scale) + shift


def _normmod_kernel(d, tiles_per_sample, ctx_row, x_ref, c_ref, mod_ref, nw_ref, xs_ref, h_ref):
    i = pl.program_id(0)
    mrow = _mod_row(i, tiles_per_sample, ctx_row)
    shift = mod_ref[pl.ds(mrow, 1), 0:d]
    scale = mod_ref[pl.ds(mrow, 1), d:2 * d]

    def emit(src_ref):
        x = src_ref[...]
        xs_ref[...] = x
        h_ref[...] = _norm_mod(x, nw_ref[...], shift, scale).astype(BF16)

    is_ctx = i % tiles_per_sample == 0
    pl.when(is_ctx)(lambda: emit(c_ref))
    pl.when(jnp.logical_not(is_ctx))(lambda: emit(x_ref))


def _norm_mod_rows(x, ctx, mods_l, norm_w):
    n_batch, n_lat, d = x.shape
    n_ctx = ctx.shape[1]
    tm = ROW_TILE
    assert n_ctx == tm
    lps = n_lat // tm
    tps = lps + 1
    ntot = n_batch * (n_ctx + n_lat)
    return pl.pallas_call(
        functools.partial(_normmod_kernel, d, tps, n_batch),
        name="norm_mod",
        grid=(n_batch * tps,),
        in_specs=[
            pl.BlockSpec((tm, d), lambda i: ((i // tps) * lps + jnp.maximum(i % tps - 1, 0), 0)),
            pl.BlockSpec((tm, d), lambda i: (i // tps, 0)),
            pl.BlockSpec((8, 2 * d), lambda i: (0, 0)),
            pl.BlockSpec((1, d), lambda i: (0, 0)),
        ],
        out_specs=[pl.BlockSpec((tm, d), lambda i: (i, 0))] * 2,
        out_shape=[jax.ShapeDtypeStruct((ntot, d), F32), jax.ShapeDtypeStruct((ntot, d), BF16)],
        compiler_params=_cparams(1),
    )(x.reshape(n_batch * n_lat, d), ctx.reshape(n_batch * n_ctx, d), mods_l, norm_w.reshape(1, d))


def _inproj_kernel(h_ref, w_ref, o_ref):
    o_ref[...] = _dot(h_ref[...], w_ref[0].astype(BF16)).astype(BF16)


def _in_proj(h, w_in, layer):
    ntot, d = h.shape
    n_cols = w_in.shape[2]
    tn = INPROJ_COLS
    tm = max(k for k in range(16, INPROJ_MAX_ROWS + 1, 16) if ntot % k == 0)
    return pl.pallas_call(
        _inproj_kernel,
        name="in_proj",
        grid=(ntot // tm, n_cols // tn),
        in_specs=[
            pl.BlockSpec((tm, d), lambda r, j: (r, 0)),
            pl.BlockSpec((1, d, tn), lambda r, j: (layer, 0, j)),
        ],
        out_specs=pl.BlockSpec((tm, tn), lambda r, j: (r, j)),
        out_shape=jax.ShapeDtypeStruct((ntot, n_cols), BF16),
        compiler_params=_cparams(2),
    )(h, w_in)


COMPACT_MIN_HALF = 8
MXU_LEVEL_MIN_HALF = 2


def _scan_levels(c):
    out, m = [], c // 2
    while m >= MXU_LEVEL_MIN_HALF:
        out.append(m)
        m //= 2
    return out


def _scan_row_layout(c):
    off = {"b": 0, "kend": c, "tot": 2 * c}
    pos = 2 * c + 8
    for m in _scan_levels(c):
        n = c // 2 if m >= COMPACT_MIN_HALF else c
        off[("q", m)] = pos
        off[("k", m)] = pos + n
        pos += 2 * n
    return off, pos


def _scan_constants(c):
    t = np.arange(c)[:, None]
    r = np.arange(c)[None, :]
    out = []
    for flip in (False, True):
        fl = (lambda a: a[::-1, ::-1]) if flip else (lambda a: a)
        sets = [fl(r <= t), fl(r > t), np.ones((8, c), bool)]
        for m in _scan_levels(c):
            mid = (t // (2 * m)) * (2 * m) + m
            qa = fl((t >= mid) & (r >= mid) & (r <= t))
            ka = fl((t < mid) & (r > t) & (r <= mid - 1))
            if m >= COMPACT_MIN_HALF:
                rows = np.arange(c)
                q_valid = (rows % (2 * m) >= m) != flip
                sets += [qa[q_valid], ka[~q_valid]]
            else:
                sets += [qa, ka]
        mat = np.concatenate(sets, axis=0).astype(np.float32)
        out.append(np.concatenate([mat, mat], axis=1))
    return np.stack(out)


def _tile_pairs(e, m, c):
    parts = []
    for p in range(c // (2 * m)):
        blk = e[p * m:(p + 1) * m]
        parts += [blk, blk]
    return jnp.concatenate(parts, axis=0)


def _scan_gates(c, q_ref, f_ref, lb_ref, mat, e_ref):
    zq = q_ref[...].astype(F32)
    zf = f_ref[...].astype(F32)
    lbv = lb_ref[0]
    f = lbv + (1.0 - lbv) * _sigmoid(zf)
    kk = 1.0 - f
    qq = zq * _sigmoid(zq)
    lf2 = jnp.log(f) * math.log2(math.e)
    lf_hi = lf2.astype(BF16)
    lf_lo = (lf2 - lf_hi.astype(F32)).astype(BF16)
    e_ref[...] = jnp.exp2(_dot(mat, jnp.concatenate([lf_hi, lf_lo], axis=0)))
    row = lax.broadcasted_iota(jnp.int32, (c, 1), 0)
    k_sw = jnp.where((row & 1) == 1, pltpu.roll(kk, 1, axis=0), pltpu.roll(kk, c - 1, axis=0))
    return qq, kk, qq * f, k_sw


def _scan_intra(n_heads, c, backward, qq, kk, qf, k_sw, e_ref, a_ref):
    levels = _scan_levels(c)
    off, _ = _scan_row_layout(c)
    ti = lax.broadcasted_iota(jnp.int32, (c, c), 0)
    si = lax.broadcasted_iota(jnp.int32, (c, c), 1)
    same = {m: (ti & -(2 * m)) == (si & -(2 * m)) for m in levels[1:] + [1]}
    eye = ti == si
    causal = (ti <= si) if backward else (ti >= si)
    for h in range(n_heads):
        cs = slice(h * HEAD_DIM, (h + 1) * HEAD_DIM)
        qh = qq[:, cs]
        kh = kk[:, cs]
        a = None
        for m in levels:
            n = c // 2 if m >= COMPACT_MIN_HALF else c
            eq = e_ref[off[("q", m)]:off[("q", m)] + n, cs]
            ek = e_ref[off[("k", m)]:off[("k", m)] + n, cs]
            if m >= COMPACT_MIN_HALF:
                eq, ek = _tile_pairs(eq, m, c), _tile_pairs(ek, m, c)
            p = _dot_nt((qh * eq).astype(BF16), (kh * ek).astype(BF16))
            a = p if a is None else jnp.where(same[m], p, a)
        pair = jnp.sum(qf[:, cs] * k_sw[:, cs], axis=-1, keepdims=True)
        diag = jnp.sum(qh * kh, axis=-1, keepdims=True)
        a = jnp.where(same[1], pair, a)
        a = jnp.where(eye, diag, a)
        a_ref[h] = jnp.where(causal, a, 0.0).astype(BF16)


def _scan_state(n_heads, c, qq, kk, i_ref, e_ref, a_ref, s_ref, o_ref):
    off, _ = _scan_row_layout(c)
    for h in range(n_heads):
        cs = slice(h * HEAD_DIM, (h + 1) * HEAD_DIM)
        qh = qq[:, cs]
        kh = kk[:, cs]
        vh = i_ref[:, cs]
        st = s_ref[h]
        qb = (qh * e_ref[off["b"]:off["b"] + c, cs]).astype(BF16)
        o = _dot_nt(qb, st.astype(BF16)) + _dot(a_ref[h], vh)
        o_ref[:, cs] = o.astype(BF16)
        vt = vh.astype(F32).T.astype(BF16)
        kt = (kh * e_ref[off["kend"]:off["kend"] + c, cs]).astype(BF16)
        s_ref[h] = st * e_ref[off["tot"]:off["tot"] + 1, cs] + _dot(vt, kt)


def _scan_kernel(n_heads, c, qf_ref, qb_ref, ff_ref, fb_ref, if_ref, ib_ref, lbf_ref, lbb_ref, m_ref,
                 of_ref, ob_ref, s_ref, e_ref, a_ref):
    @pl.when(pl.program_id(1) == 0)
    def _():
        s_ref[...] = jnp.zeros_like(s_ref)

    dirs = ((qf_ref, ff_ref, if_ref, lbf_ref, of_ref), (qb_ref, fb_ref, ib_ref, lbb_ref, ob_ref))
    work = [(d, pl.ds((SCAN_GROUP - 1 - k if d else k) * c, c)) for d in range(2) for k in range(SCAN_GROUP)]
    gates = []
    for n, (d, rows) in enumerate(work):
        q, f, _, lb, _ = dirs[d]
        gates.append(_scan_gates(c, q.at[rows], f.at[rows], lb, m_ref[d], e_ref.at[n]))
    for n, (d, rows) in enumerate(work):
        _scan_intra(n_heads, c, d == 1, *gates[n], e_ref.at[n], a_ref.at[n])
    for n, (d, rows) in enumerate(work):
        _, _, i_ref, _, o_ref = dirs[d]
        _scan_state(n_heads, c, gates[n][0], gates[n][1], i_ref.at[rows], e_ref.at[n], a_ref.at[n],
                    s_ref.at[d], o_ref.at[rows])


def _scan(z, lb, layer, depth, n_batch, n_ctx, n_lat, hw):
    c = SCAN_CHUNK
    blk = SCAN_GROUP * c
    n_heads = hw // HEAD_DIM
    assert n_ctx % blk == 0 and n_lat % blk == 0
    nc, nl = n_ctx // blk, n_lat // blk
    steps = nc + nl
    mats = jnp.asarray(_scan_constants(c), BF16)
    n_rows = mats.shape[1]
    assert n_rows == _scan_row_layout(c)[1]
    ntot = z.shape[0]

    def fwd(b, s):
        return b * steps + s

    def bwd(b, s):
        return b * steps + jnp.where(s < nc, nc - 1 - s, nc + (steps - 1 - s))

    def col(chunk, j):
        return pl.BlockSpec((blk, hw), lambda b, s: (chunk(b, s), j))

    lb_spec = lambda d: pl.BlockSpec((1, 1, hw), lambda b, s: (d * depth + layer, 0, 0))
    return pl.pallas_call(
        functools.partial(_scan_kernel, n_heads, c),
        name="hgrn2_scan",
        grid=(n_batch, steps),
        in_specs=[
            col(fwd, 1), col(bwd, 1),
            col(fwd, 2), col(bwd, 3),
            col(fwd, 4), col(bwd, 4),
            lb_spec(0), lb_spec(1),
            pl.BlockSpec((2, n_rows, 2 * c), lambda b, s: (0, 0, 0)),
        ],
        out_specs=[pl.BlockSpec((blk, hw), lambda b, s: (fwd(b, s), 0)),
                   pl.BlockSpec((blk, hw), lambda b, s: (bwd(b, s), 0))],
        out_shape=[jax.ShapeDtypeStruct((ntot, hw), BF16)] * 2,
        scratch_shapes=[pltpu.VMEM((2, n_heads, HEAD_DIM, HEAD_DIM), F32),
                        pltpu.VMEM((2 * SCAN_GROUP, n_rows, hw), F32),
                        pltpu.VMEM((2 * SCAN_GROUP, n_heads, c, c), BF16)],
        compiler_params=_cparams(2),
    )(z, z, z, z, z, z, lb, lb, mats)


def _pool_constants(tile, n_ctx):
    assert n_ctx == tile and tile % GRID_W == 0
    band = np.zeros((2, len(POOL_WINDOWS), tile, tile), np.float32)
    inv = np.zeros((2, len(POOL_WINDOWS), tile, 1), np.float32)
    for kind, row_len in enumerate((n_ctx, GRID_W)):
        t = np.arange(tile)
        tau = t % row_len
        base = t - tau
        for g, w in enumerate(POOL_WINDOWS):
            lo = np.maximum(tau - w // 2, 0)
            hi = np.minimum(tau + w // 2 - 1, row_len - 1)
            s = np.arange(tile)[None, :]
            band[kind, g] = (s >= (base + lo)[:, None]) & (s <= (base + hi)[:, None])
            inv[kind, g, :, 0] = 1.0 / (hi - lo + 1)
    return band, inv


def _route(sel, sc):
    n_g, per = N_EXPERT_GROUPS, EXPERTS_PER_GROUP
    row = lambda a, e: a[e:e + 1, :]
    best = None
    for g in range(n_g):
        rows = [row(sel, g * per + i) for i in range(per)]
        gs = None
        for i in range(per):
            for j in range(i + 1, per):
                p = rows[i] + rows[j]
                gs = p if gs is None else jnp.maximum(gs, p)
        if best is None:
            best, g_idx = gs, jnp.zeros(gs.shape, jnp.int32)
        else:
            upd = gs > best
            g_idx = jnp.where(upd, g, g_idx)
            best = jnp.where(upd, gs, best)
    cand, raw = [], []
    for i in range(per):
        ci, ri = row(sel, i), row(sc, i)
        for g in range(1, n_g):
            ci = jnp.where(g_idx == g, row(sel, g * per + i), ci)
            ri = jnp.where(g_idx == g, row(sc, g * per + i), ri)
        cand.append(ci)
        raw.append(ri)

    def argmax4(vals):
        m, loc = vals[0], jnp.zeros(vals[0].shape, jnp.int32)
        for i in range(1, per):
            upd = vals[i] > m
            loc = jnp.where(upd, i, loc)
            m = jnp.where(upd, vals[i], m)
        return loc

    loc0 = argmax4(cand)
    loc1 = argmax4([jnp.where(loc0 == i, -jnp.inf, cand[i]) for i in range(per)])

    def pick(loc):
        out = raw[0]
        for i in range(1, per):
            out = jnp.where(loc == i, raw[i], out)
        return out

    g0, g1 = pick(loc0), pick(loc1)
    tot = g0 + g1
    return g_idx * per + loc0, g_idx * per + loc1, g0 / tot, g1 / tot


def _merge_kernel(d, hw, tiles_per_sample, ctx_row,
                  x_ref, p_ref, og_ref, gp_ref, gh_ref, of_ref, ob_ref, mod_ref, n2_ref, ps_ref, hn_ref,
                  band_ref, inv_ref, wpg_ref, wbp_ref, wbh_ref, wout_ref, wr_ref, rb_ref, tri_ref,
                  xo_ref, h2_ref, eidx_ref, gate_ref, cnt_ref, pool_scr, hg_scr, carry_ref):
    i = pl.program_id(0)
    n_heads = hw // HEAD_DIM
    n_groups = len(POOL_WINDOWS)
    pg = p_ref.shape[1] // n_groups
    mrow = _mod_row(i, tiles_per_sample, ctx_row)
    mod = lambda k: mod_ref[pl.ds(mrow, 1), k * d:(k + 1) * d]

    for h in range(n_heads):
        cs = slice(h * HEAD_DIM, (h + 1) * HEAD_DIM)
        o = of_ref[:, cs].astype(F32) + ob_ref[:, cs].astype(F32)
        o = o * lax.rsqrt(jnp.mean(o * o, axis=-1, keepdims=True) + EPS)
        og = og_ref[:, cs].astype(F32)
        hg_scr[:, cs] = (o * hn_ref[:, cs] * (og * _sigmoid(og))).astype(BF16)

    for g in range(n_groups):
        cs = slice(g * pg, (g + 1) * pg)
        u = p_ref[:, cs]
        wsum = _dot(band_ref[0, g], u)
        dlt = wsum * inv_ref[0, g] - u.astype(F32)
        pool_scr[:, cs] = (_dot(dlt.astype(BF16), wpg_ref[g]) * ps_ref[:, cs]).astype(BF16)

    gp = gp_ref[...].astype(F32)
    gh = gh_ref[...].astype(F32)
    m = _sigmoid(gp) * _dot(pool_scr[...], wbp_ref[...]) + _sigmoid(gh) * _dot(hg_scr[...], wbh_ref[...])
    y = _dot(m.astype(BF16), wout_ref[...])
    xn = x_ref[...] + mod(2) * y
    xo_ref[...] = xn
    n2 = xn * lax.rsqrt(jnp.mean(xn * xn, axis=-1, keepdims=True) + EPS) * n2_ref[...]
    h2 = n2 * (1.0 + mod(4)) + mod(3)
    h2_ref[...] = _pack_halves(h2)

    sc = _sigmoid(_dot_nt(wr_ref[...].astype(BF16), h2.astype(BF16)))
    e0, e1, g0, g1 = _route(sc + rb_ref[...], sc)

    @pl.when(i == 0)
    def _():
        carry_ref[...] = jnp.zeros_like(carry_ref)

    er = lax.broadcasted_iota(jnp.int32, sc.shape, 0)
    oh0 = (er == e0).astype(F32)
    oh1 = (er == e1).astype(F32)
    both = oh0 + oh1
    base = carry_ref[:, 0:1] + _dot(both.astype(BF16), tri_ref[...])
    r0 = jnp.sum(oh0 * base, axis=0, keepdims=True)
    r1 = jnp.sum(oh1 * base, axis=0, keepdims=True)
    carry_ref[...] = carry_ref[...] + jnp.sum(both, axis=1, keepdims=True)
    cnt_ref[...] = carry_ref[...].astype(jnp.int32)

    eidx_ref[...] = jnp.zeros_like(eidx_ref)
    gate_ref[...] = jnp.zeros_like(gate_ref)
    eidx_ref[0, 0:1, :] = e0
    eidx_ref[0, 1:2, :] = e1
    eidx_ref[0, 2:3, :] = r0.astype(jnp.int32)
    eidx_ref[0, 3:4, :] = r1.astype(jnp.int32)
    gate_ref[0, 0:1, :] = g0
    gate_ref[0, 1:2, :] = g1


def _const_spec(shape):
    nd = len(shape)
    return pl.BlockSpec(shape, lambda i: (0,) * nd, pipeline_mode=pl.Buffered(1))


def _merge(x, z, o_f, o_b, mods_l, norm2_w, pool_scale, hg_norm_w, w_pg, w_bp, w_bh, w_out, w_router_t,
           router_bias, n_ctx, t_sample, ctx_row):
    ntot, d = x.shape
    hw = o_f.shape[1]
    d_pool = w_bp.shape[0]
    assert d_pool == hw and d == 2 * hw
    tm = ROW_TILE
    tps = t_sample // tm
    n_e = w_router_t.shape[0]
    band, inv = _pool_constants(tm, n_ctx)
    band = jnp.asarray(band, BF16)
    inv = jnp.asarray(inv, F32)
    kind = lambda i: jnp.where(i % tps == 0, 0, 1)
    n_tiles = ntot // tm
    before = jnp.asarray(np.triu(np.ones((tm, tm), np.float32), 1), BF16)
    return pl.pallas_call(
        functools.partial(_merge_kernel, d, hw, tps, ctx_row),
        name="merge",
        grid=(n_tiles,),
        in_specs=[
            pl.BlockSpec((tm, d), lambda s: (s, 0)),
            pl.BlockSpec((tm, hw), lambda s: (s, 0)),
            pl.BlockSpec((tm, hw), lambda s: (s, 5)),
            pl.BlockSpec((tm, d), lambda s: (s, 3)),
            pl.BlockSpec((tm, d), lambda s: (s, 4)),
            pl.BlockSpec((tm, hw), lambda s: (s, 0)),
            pl.BlockSpec((tm, hw), lambda s: (s, 0)),
            _const_spec((8, 6 * d)),
            _const_spec((1, d)),
            _const_spec((1, hw)),
            _const_spec((1, hw)),
            pl.BlockSpec((1,) + band.shape[1:], lambda s: (kind(s), 0, 0, 0)),
            pl.BlockSpec((1,) + inv.shape[1:], lambda s: (kind(s), 0, 0, 0)),
            _const_spec(w_pg.shape),
            _const_spec(w_bp.shape),
            _const_spec(w_bh.shape),
            _const_spec(w_out.shape),
            _const_spec(w_router_t.shape),
            _const_spec((n_e, 1)),
            _const_spec((tm, tm)),
        ],
        out_specs=[
            pl.BlockSpec((tm, d), lambda s: (s, 0)),
            pl.BlockSpec((tm, d // 2), lambda s: (s, 0)),
            pl.BlockSpec((1, 8, tm), lambda s: (s, 0, 0)),
            pl.BlockSpec((1, 8, tm), lambda s: (s, 0, 0)),
            pl.BlockSpec((n_e, 128), lambda s: (0, 0)),
        ],
        out_shape=[
            jax.ShapeDtypeStruct((ntot, d), F32),
            jax.ShapeDtypeStruct((ntot, d // 2), jnp.uint32),
            jax.ShapeDtypeStruct((n_tiles, 8, tm), jnp.int32),
            jax.ShapeDtypeStruct((n_tiles, 8, tm), F32),
            jax.ShapeDtypeStruct((n_e, 128), jnp.int32),
        ],
        scratch_shapes=[pltpu.VMEM((tm, hw), BF16), pltpu.VMEM((tm, hw), BF16), pltpu.VMEM((n_e, 128), F32)],
        compiler_params=_cparams(1),
    )(x, z, z, z, z, o_f, o_b, mods_l, norm2_w.reshape(1, d), pool_scale.reshape(1, hw),
      hg_norm_w.reshape(1, hw), band, inv, w_pg, w_bp, w_bh, w_out, w_router_t,
      router_bias.astype(F32).reshape(n_e, 1), before)


def _row_gather_start(src_hbm, idx_ref, dst, sem, n_rows):
    for r in range(n_rows):
        tok = idx_ref[0, 0, r]
        pltpu.make_async_copy(src_hbm.at[pl.ds(tok, 1)], dst.at[pl.ds(r, 1)], sem).start(priority=r % 2)


def _row_gather_wait(src_hbm, dst, sem, n_rows):
    pltpu.make_async_copy(src_hbm.at[pl.ds(0, n_rows)], dst, sem).wait()


def _dispatch_kernel(n_e, n_blk, tails_ref, pos_ref, h_ref, xs_hbm, zblk, sems):
    tm = h_ref.shape[0]
    rows = zblk.shape[0]
    sem = sems.at[0]
    zsem = sems.at[1]

    def zero_block(b):
        return pltpu.make_async_copy(zblk, xs_hbm.at[pl.ds(pl.multiple_of(b * rows, rows), rows)], zsem)

    def for_each_zero_block(act):
        for e in range(n_e):
            pl.when(tails_ref[e] >= 0)(lambda e=e: act(zero_block(tails_ref[e])))

        def blk_body(b, carry):
            act(zero_block(b))
            return carry

        lax.fori_loop(tails_ref[n_e], n_blk, blk_body, 0)

    @pl.when(pl.program_id(0) == 0)
    def _():
        zblk[...] = jnp.zeros_like(zblk)
        for_each_zero_block(lambda cp: cp.start())
        for_each_zero_block(lambda cp: cp.wait())

    for r in range(tm):
        for k in range(2):
            dst = pos_ref[0, 0, k * tm + r]
            pltpu.make_async_copy(h_ref.at[pl.ds(r, 1)], xs_hbm.at[pl.ds(dst, 1)], sem).start(priority=k)
    pltpu.make_async_copy(xs_hbm.at[pl.ds(0, 2 * tm)], xs_hbm.at[pl.ds(0, 2 * tm)], sem).wait()


def _dispatch(h2, pos, tails, n_blk):
    ntot, d = h2.shape
    tm = ROW_TILE
    n_tiles = ntot // tm
    n_e = tails.shape[0] - 1
    grid_spec = pltpu.PrefetchScalarGridSpec(
        num_scalar_prefetch=1,
        grid=(n_tiles,),
        in_specs=[
            pl.BlockSpec((1, 1, 2 * tm), lambda i, t: (i, 0, 0), memory_space=pltpu.SMEM),
            pl.BlockSpec((tm, d), lambda i, t: (i, 0)),
        ],
        out_specs=pl.BlockSpec(memory_space=pl.ANY),
        scratch_shapes=[pltpu.VMEM((MOE_ROWS, d), h2.dtype), pltpu.SemaphoreType.DMA((2,))],
    )
    return pl.pallas_call(
        functools.partial(_dispatch_kernel, n_e, n_blk),
        name="dispatch",
        grid_spec=grid_spec,
        out_shape=jax.ShapeDtypeStruct((n_blk * MOE_ROWS, d), h2.dtype),
        compiler_params=_cparams(1),
    )(tails, pos, h2)


def _expert_kernel(blk_e_ref, nused_ref, x_ref, w1_ref, w3_ref, w2_ref, y_ref, w1c, w3c, w2c):
    i = pl.program_id(0)
    n_used = nused_ref[0]

    @pl.when(i < n_used)
    def _():
        changed = jnp.logical_or(i == 0, blk_e_ref[i] != blk_e_ref[jnp.maximum(i - 1, 0)])

        @pl.when(changed)
        def _():
            w1c[...] = w1_ref[0, 0].astype(BF16)
            w3c[...] = w3_ref[0, 0].astype(BF16)
            w2c[...] = w2_ref[0, 0].astype(BF16)

        xb = jnp.concatenate(_unpack_halves(x_ref[...]), axis=1).astype(BF16)
        h1 = _dot(xb, w1c[...])
        h3 = _dot(xb, w3c[...])
        act = (h1 * _sigmoid(h1) * h3).astype(BF16)
        y_ref[...] = _pack_halves(_dot(act, w2c[...]))

    @pl.when(i >= n_used)
    def _():
        y_ref[...] = jnp.zeros_like(y_ref)


def _experts(xs, blk_e, n_used, w1, w3, w2, layer):
    n_rows = xs.shape[0]
    _, n_e, d, f = w1.shape
    n_blk = blk_e.shape[0]
    rows = MOE_ROWS
    used = lambda i, nu: jnp.minimum(i, nu[0] - 1)
    grid_spec = pltpu.PrefetchScalarGridSpec(
        num_scalar_prefetch=2,
        grid=(n_blk,),
        in_specs=[
            pl.BlockSpec((rows, d // 2), lambda i, be, nu: (used(i, nu), 0)),
            pl.BlockSpec((1, 1, d, f), lambda i, be, nu: (layer, be[i], 0, 0)),
            pl.BlockSpec((1, 1, d, f), lambda i, be, nu: (layer, be[i], 0, 0)),
            pl.BlockSpec((1, 1, f, d), lambda i, be, nu: (layer, be[i], 0, 0)),
        ],
        out_specs=pl.BlockSpec((rows, d // 2), lambda i, be, nu: (i, 0)),
        scratch_shapes=[
            pltpu.VMEM((d, f), BF16),
            pltpu.VMEM((d, f), BF16),
            pltpu.VMEM((f, d), BF16),
        ],
    )
    return pl.pallas_call(
        _expert_kernel,
        name="experts",
        grid_spec=grid_spec,
        out_shape=jax.ShapeDtypeStruct((n_rows, d // 2), jnp.uint32),
        compiler_params=_cparams(1),
    )(blk_e, n_used, xs, w1, w3, w2)


def _combine_kernel(d, mode, tile_of, pos0_ref, posn_ref, y_hbm, x_ref, g_ref, mod_ref, *rest):
    if mode == "next":
        modn_ref, nw_ref, xo_ref, h_ref, rbuf, sem = rest
    else:
        fw_ref, out_ref, rbuf, sem = rest
    i = pl.program_id(0)
    n = pl.num_programs(0)
    tm = x_ref.shape[0]
    slot = i % 2

    @pl.when(i == 0)
    def _():
        _row_gather_start(y_hbm, pos0_ref, rbuf.at[0], sem.at[0], 2 * tm)

    @pl.when(i + 1 < n)
    def _():
        _row_gather_start(y_hbm, posn_ref, rbuf.at[1 - slot], sem.at[1 - slot], 2 * tm)

    _row_gather_wait(y_hbm, rbuf.at[slot], sem.at[slot], 2 * tm)
    mrow = tile_of(i)[1]
    m5 = mod_ref[pl.ds(mrow, 1), 5 * d:6 * d]
    g = g_ref[...]
    lo0, hi0 = _unpack_halves(rbuf[slot, 0:tm, :])
    lo1, hi1 = _unpack_halves(rbuf[slot, tm:2 * tm, :])
    g0, g1 = g[:, 0:1], g[:, 1:2]
    f = jnp.concatenate([g0 * lo0 + g1 * lo1, g0 * hi0 + g1 * hi1], axis=1)
    xn = x_ref[...] + m5 * f
    if mode == "next":
        xo_ref[...] = xn
        shift = modn_ref[pl.ds(mrow, 1), 0:d]
        scale = modn_ref[pl.ds(mrow, 1), d:2 * d]
        h_ref[...] = _norm_mod(xn, nw_ref[...], shift, scale).astype(BF16)
    else:
        out_ref[...] = xn * lax.rsqrt(jnp.mean(xn * xn, axis=-1, keepdims=True) + EPS) * fw_ref[...]


def _combine(x, yb, pos, gates, mods_l, n_batch, n_ctx, n_lat, mods_next=None, norm_next=None, final_w=None):
    ntot, d = x.shape
    tm = ROW_TILE
    tps, cps, lps = (n_ctx + n_lat) // tm, n_ctx // tm, n_lat // tm
    mode = "final" if final_w is not None else "next"
    if mode == "final":
        n_steps = n_batch * lps

        def tile_of(i):
            return (i // lps) * tps + cps + i % lps, i // lps
    else:
        n_steps = ntot // tm

        def tile_of(i):
            return i, _mod_row(i, tps, n_batch)

    tile = lambda i: tile_of(i)[0]
    in_specs = [
        pl.BlockSpec((1, 1, 2 * tm), lambda i: (tile(0), 0, 0), memory_space=pltpu.SMEM),
        pl.BlockSpec((1, 1, 2 * tm), lambda i: (tile(jnp.minimum(i + 1, n_steps - 1)), 0, 0),
                     memory_space=pltpu.SMEM),
        pl.BlockSpec(memory_space=pl.ANY),
        pl.BlockSpec((tm, d), lambda i: (tile(i), 0)),
        pl.BlockSpec((tm, 2), lambda i: (tile(i), 0)),
        _const_spec((8, 6 * d)),
    ]
    args = [pos, pos, yb, x, gates, mods_l]
    if mode == "next":
        in_specs += [pl.BlockSpec((8, 2 * d), lambda i: (0, 0)), pl.BlockSpec((1, d), lambda i: (0, 0))]
        args += [mods_next, norm_next.reshape(1, d)]
        out_specs = [pl.BlockSpec((tm, d), lambda i: (i, 0))] * 2
        out_shape = [jax.ShapeDtypeStruct((ntot, d), F32), jax.ShapeDtypeStruct((ntot, d), BF16)]
    else:
        in_specs += [pl.BlockSpec((1, d), lambda i: (0, 0))]
        args += [final_w.reshape(1, d)]
        out_specs = pl.BlockSpec((tm, d), lambda i: (i, 0))
        out_shape = jax.ShapeDtypeStruct((n_steps * tm, d), F32)
    return pl.pallas_call(
        functools.partial(_combine_kernel, d, mode, tile_of),
        name="combine",
        grid=(n_steps,),
        in_specs=in_specs,
        out_specs=out_specs,
        out_shape=out_shape,
        scratch_shapes=[pltpu.VMEM((2, 2 * tm, d // 2), jnp.uint32), pltpu.SemaphoreType.DMA((2,))],
        compiler_params=_cparams(1),
    )(*args)


def _dispatch_plan(eidx, counts, n_e):
    rows = MOE_ROWS
    n_tiles, _, tm = eidx.shape
    n_assign = n_tiles * tm * 2
    pcounts = (counts + rows - 1) // rows * rows
    pends = jnp.cumsum(pcounts)
    pstart = pends - pcounts
    experts = jnp.arange(n_e, dtype=jnp.int32)
    first = jnp.sum(jnp.where(eidx[:, 0:2, :, None] == experts, pstart, 0), axis=-1)
    dest = first + eidx[:, 2:4, :]
    n_blk = -(-n_assign // rows) + n_e
    blk_row0 = jnp.arange(n_blk, dtype=jnp.int32) * rows
    blk_e = jnp.minimum(jnp.sum(blk_row0[:, None] >= pends[None, :], axis=1), n_e - 1).astype(jnp.int32)
    n_used = (pends[-1] // rows).astype(jnp.int32).reshape(1)
    pos = dest.astype(jnp.int32).reshape(n_tiles, 1, 2 * tm)
    tails = jnp.concatenate([jnp.where(pcounts > counts, pends // rows - 1, -1), n_used]).astype(jnp.int32)
    return blk_e, n_used, pos, n_blk, tails


def kernel(x, c, ctx, c_ctx, w_ada, b_ada, norm1_w, norm2_w, w_in, w_pool_group, pool_scale, hg_norm_w,
           lb_logits, w_branch_pool, w_branch_hgrn, w_out, w_router, router_bias, w_e1, w_e3, w_e2,
           final_norm_w):
    n_batch, n_lat, d = x.shape
    n_ctx = ctx.shape[1]
    depth = w_ada.shape[0]
    hw = hg_norm_w.shape[1]
    n_e = w_router.shape[1]
    t_sample = n_ctx + n_lat
    ntot = n_batch * t_sample
    ctx_row = n_batch
    assert n_batch < 8 and n_ctx == ROW_TILE and n_lat % ROW_TILE == 0

    cc = jnp.zeros((8, d), F32).at[:n_batch].set(c).at[ctx_row].set(c_ctx)
    mods = _ada_mods(cc, w_ada, b_ada)
    lb = _lower_bounds(lb_logits)
    w_router_t = jnp.transpose(w_router.astype(F32))

    xs, h = _norm_mod_rows(x, ctx, mods[0], norm1_w[0])
    for l in range(depth):
        z = _in_proj(h, w_in, l)
        o_f, o_b = _scan(z, lb, l, depth, n_batch, n_ctx, n_lat, hw)
        xs, h2, eidx, gate, counts = _merge(
            xs, z, o_f, o_b, mods[l], norm2_w[l], pool_scale[l], hg_norm_w[l], w_pool_group[l].astype(BF16),
            w_branch_pool[l].astype(BF16), w_branch_hgrn[l].astype(BF16), w_out[l].astype(BF16),
            w_router_t, router_bias, n_ctx, t_sample, ctx_row)
        blk_e, n_used, pos, n_blk, tails = _dispatch_plan(eidx, counts[:, 0], n_e)
        yb = _experts(_dispatch(h2, pos, tails, n_blk), blk_e, n_used, w_e1, w_e3, w_e2, l)
        g = jnp.transpose(gate[:, 0:2, :], (0, 2, 1)).reshape(ntot, 2)
        if l + 1 < depth:
            xs, h = _combine(xs, yb, pos, g, mods[l], n_batch, n_ctx, n_lat,
                             mods_next=mods[l + 1], norm_next=norm1_w[l + 1])
        else:
            out = _combine(xs, yb, pos, g, mods[l], n_batch, n_ctx, n_lat, final_w=final_norm_w)
    return out.reshape(n_batch, n_lat, d)
```
